```python
import math
import jax, jax.numpy as jnp
from jax import lax
import numpy as np

D_MODEL = 1024
BATCH = 16
SEQ = 2048
DEPTH = 2
DEC_BATCH = 8
DEC_SEQ = 16
PAST_LEN = 1024

CHUNK = 64
MIX_WIDTH = 2 * D_MODEL
GROUP_WIDTH = MIX_WIDTH // 4
EPS = 1e-6
NEG_INIT = -1e30

M_HEADS = 4
M_HD = GROUP_WIDTH // M_HEADS
M_QK = M_HD // 2
R_HEADS = 4
R_HD = GROUP_WIDTH // R_HEADS
R_QK = R_HD // 2
ROPE_BASE = 10000.0
S_HD = 64
S_HEADS = GROUP_WIDTH // S_HD
S_GROUPS = 2
S_STATE = 128
CONV_W = 4
S_CONV_DIM = GROUP_WIDTH + 2 * S_GROUPS * S_STATE
H_HEADS = 4
H_HD = GROUP_WIDTH // H_HEADS
H_KD = 128
H_KW = H_HEADS * H_KD
H_BLOCK = 16

IN_SIZES = (
    M_HEADS * M_QK, M_HEADS * M_QK, GROUP_WIDTH, GROUP_WIDTH, GROUP_WIDTH, 2 * M_HEADS,
    R_HEADS * R_QK, R_HEADS * R_QK, GROUP_WIDTH, GROUP_WIDTH,
    GROUP_WIDTH, S_CONV_DIM, S_HEADS,
    H_KW, H_KW, GROUP_WIDTH, GROUP_WIDTH,
)
IN_COLS = sum(IN_SIZES)

kernel_name = "hybrid_stream_encoder_step"


def rmsnorm(x, w):
    xf = x.astype(jnp.float32)
    y = xf * lax.rsqrt(jnp.mean(xf * xf, axis=-1, keepdims=True) + EPS)
    return (y * w.astype(jnp.float32)).astype(x.dtype)


def head_rms(h):
    return h * lax.rsqrt(jnp.mean(h * h, axis=-1, keepdims=True) + EPS)


def head_layernorm(h):
    c = h - jnp.mean(h, axis=-1, keepdims=True)
    return c * lax.rsqrt(jnp.mean(c * c, axis=-1, keepdims=True) + EPS)


def block_len(t, l):
    return l if t % l == 0 else t


def to_blocks(a, l):
    b, t = a.shape[:2]
    a = a.reshape((b, t // l, l) + a.shape[2:])
    return jnp.moveaxis(a, (1, 0, 3, 2), (0, 1, 2, 3))


def from_blocks(a):
    nc, b, h, l = a.shape[:4]
    a = jnp.moveaxis(a, (0, 1, 2, 3), (1, 0, 3, 2))
    return a.reshape((b, nc * l, h) + a.shape[4:])


def causal_tril(l):
    return jnp.tril(jnp.ones((l, l), dtype=bool))


def rotary(x, pos):
    half = x.shape[-1] // 2
    freqs = ROPE_BASE ** (-jnp.arange(half, dtype=jnp.float32) / half)
    ang = pos.astype(jnp.float32)[:, None] * freqs[None, :]
    cos = jnp.cos(ang)[None, :, None, :]
    sin = jnp.sin(ang)[None, :, None, :]
    x1, x2 = x[..., :half], x[..., half:]
    return jnp.concatenate([x1 * cos - x2 * sin, x1 * sin + x2 * cos], axis=-1)


def causal_dwconv(u, buf, w, b):
    up = jnp.concatenate([buf.astype(u.dtype), u], axis=1)
    y = lax.conv_general_dilated(up, w.astype(u.dtype)[:, None, :], window_strides=(1,), padding="VALID",
                                 dimension_numbers=("NWC", "WIO", "NWC"), feature_group_count=u.shape[-1])
    return jax.nn.silu(y + b.astype(u.dtype)), up[:, -(CONV_W - 1):]


def mlstm_block(carry, xs):
    c_prev, n_prev, m_prev = carry
    q, k, v, ig, lf = xs
    l = q.shape[2]
    b = jnp.cumsum(lf, axis=-1)
    d = jnp.where(causal_tril(l), b[..., :, None] - b[..., None, :] + ig[..., None, :], -jnp.inf)
    inter = b + m_prev[..., None]
    m_t = jnp.maximum(inter, jnp.max(d, axis=-1))
    s = jnp.einsum("bhlk,bhsk->bhls", q, k) * jnp.exp(d - m_t[..., None])
    wi = jnp.exp(inter - m_t)
    num = jnp.einsum("bhls,bhsv->bhlv", s, v) + wi[..., None] * jnp.einsum("bhlk,bhkv->bhlv", q, c_prev)
    den = jnp.sum(s, axis=-1) + wi * jnp.einsum("bhlk,bhk->bhl", q, n_prev)
    h = num / jnp.maximum(jnp.abs(den), jnp.exp(-m_t))[..., None]
    g = b[..., -1:] - b + ig
    m_new = jnp.maximum(b[..., -1] + m_prev, jnp.max(g, axis=-1))
    wk = jnp.exp(g - m_new[..., None])
    wc = jnp.exp(b[..., -1] + m_prev - m_new)
    c_new = wc[..., None, None] * c_prev + jnp.einsum("bhs,bhsk,bhsv->bhkv", wk, k, v)
    n_new = wc[..., None] * n_prev + jnp.einsum("bhs,bhsk->bhk", wk, k)
    return (c_new, n_new, m_new), h


def retention_block(s_prev, xs):
    q, k, v = xs
    l = q.shape[2]
    lg = jnp.log1p(-(2.0 ** (-5.0 - jnp.arange(R_HEADS, dtype=jnp.float32))))
    idx = jnp.arange(l, dtype=jnp.float32)
    dec = jnp.exp(jnp.where(causal_tril(l), (idx[:, None] - idx[None, :])[None] * lg[:, None, None], -jnp.inf))
    s = jnp.einsum("bhlk,bhsk->bhls", q, k) * dec
    o = (jnp.einsum("bhls,bhsv->bhlv", s, v)
         + jnp.exp((idx[None] + 1.0) * lg[:, None])[..., None] * jnp.einsum("bhlk,bhkv->bhlv", q, s_prev))
    kw = k * jnp.exp((l - 1.0 - idx)[None] * lg[:, None])[..., None]
    s_new = jnp.exp(l * lg)[:, None, None] * s_prev + jnp.einsum("bhsk,bhsv->bhkv", kw, v)
    return s_new, o


def ssd_block(h_prev, xs):
    cq, bk, xv, la = xs
    bsz, g, l, n = cq.shape
    hg = S_HEADS // S_GROUPS
    b = jnp.cumsum(la, axis=-1)
    dec = jnp.exp(jnp.where(causal_tril(l), b[..., :, None] - b[..., None, :], -jnp.inf))
    cb = jnp.einsum("bgln,bgsn->bgls", cq, bk)
    scores = cb[:, :, None] * dec.reshape(bsz, g, hg, l, l)
    xg = xv.reshape(bsz, g, hg, l, S_HD)
    hp = h_prev.reshape(bsz, g, hg, n, S_HD)
    y = (jnp.einsum("bgkls,bgksp->bgklp", scores, xg)
         + jnp.exp(b).reshape(bsz, g, hg, l)[..., None] * jnp.einsum("bgln,bgknp->bgklp", cq, hp))
    xw = (xv * jnp.exp(b[..., -1:] - b)[..., None]).reshape(bsz, g, hg, l, S_HD)
    h_new = (jnp.exp(b[..., -1])[..., None, None] * h_prev
             + jnp.einsum("bgsn,bgksp->bgknp", bk, xw).reshape(bsz, S_HEADS, n, S_HD))
    return h_new, y.reshape(bsz, S_HEADS, l, S_HD)


def hgrn_block(s_prev, xs):
    q, k, v, lf = xs
    l = q.shape[2]
    b = jnp.cumsum(lf, axis=2)
    diff = b[:, :, :, None, :] - b[:, :, None, :, :]
    w = jnp.exp(jnp.where(causal_tril(l)[:, :, None], diff, -jnp.inf))
    s = jnp.einsum("bhtk,bhsk,bhtsk->bhts", q, k, w)
    o = jnp.einsum("bhts,bhsv->bhtv", s, v) + jnp.einsum("bhtk,bhkv->bhtv", q * jnp.exp(b), s_prev)
    bl = b[:, :, -1]
    s_new = jnp.exp(bl)[..., None] * s_prev + jnp.einsum("bhsk,bhsv->bhkv", k * jnp.exp(bl[:, :, None, :] - b), v)
    return s_new, o


def mixer_layer(x, pos, states, norm_w, w_in, m_gate_b, m_norm_w, conv_w, conv_b,
                dt_bias, a_log, d_skip, s_norm_w, lb, h_norm_w, w_out):
    f32 = jnp.float32
    bsz, t, _ = x.shape
    m_c, m_n, m_m, r_s, s_h, s_buf, h_s = [s.astype(f32) for s in states]
    hn = rmsnorm(x, norm_w)
    proj = jnp.einsum("btd,dc->btc", hn, w_in).astype(f32)
    cuts = np.cumsum(IN_SIZES)[:-1].tolist()
    (mq, mk, mv, mo, mz, mif, rq, rk, rv, rg, sz, sxbc, sdt, hq, hf, hi, hg) = jnp.split(proj, cuts, axis=-1)

    lc = block_len(t, CHUNK)
    q = mq.reshape(bsz, t, M_HEADS, M_QK) * (M_QK ** -0.5)
    k = mk.reshape(bsz, t, M_HEADS, M_QK)
    v = mv.reshape(bsz, t, M_HEADS, M_HD)
    gates = (mif + m_gate_b.astype(f32)).reshape(bsz, t, 2, M_HEADS)
    ig = gates[:, :, 0]
    lf = jax.nn.log_sigmoid(gates[:, :, 1])
    (m_c, m_n, m_m), hm = lax.scan(mlstm_block, (m_c, m_n, m_m),
                                   (to_blocks(q, lc), to_blocks(k, lc), to_blocks(v, lc),
                                    to_blocks(ig, lc), to_blocks(lf, lc)))
    hm = head_rms(from_blocks(hm)) * m_norm_w.astype(f32).reshape(M_HEADS, M_HD)
    out_m = hm.reshape(bsz, t, GROUP_WIDTH) * jax.nn.sigmoid(mo) * jax.nn.silu(mz)

    q = rotary(rq.reshape(bsz, t, R_HEADS, R_QK), pos) * (R_QK ** -0.5)
    k = rotary(rk.reshape(bsz, t, R_HEADS, R_QK), pos)
    v = rv.reshape(bsz, t, R_HEADS, R_HD)
    r_s, hr = lax.scan(retention_block, r_s, (to_blocks(q, lc), to_blocks(k, lc), to_blocks(v, lc)))
    out_r = head_layernorm(from_blocks(hr)).reshape(bsz, t, GROUP_WIDTH) * jax.nn.silu(rg)

    xbc, s_buf = causal_dwconv(sxbc, s_buf, conv_w, conv_b)
    xs_, bs, cs = jnp.split(xbc, [GROUP_WIDTH, GROUP_WIDTH + S_GROUPS * S_STATE], axis=-1)
    xs_ = xs_.reshape(bsz, t, S_HEADS, S_HD)
    bs = bs.reshape(bsz, t, S_GROUPS, S_STATE)
    cs = cs.reshape(bsz, t, S_GROUPS, S_STATE)
    dt = jax.nn.softplus(sdt + dt_bias.astype(f32))
    la = dt * (-jnp.exp(a_log.astype(f32)))
    s_h, ys = lax.scan(ssd_block, s_h, (to_blocks(cs, lc), to_blocks(bs, lc),
                                        to_blocks(xs_ * dt[..., None], lc), to_blocks(la, lc)))
    ys = from_blocks(ys) + d_skip.astype(f32)[:, None] * xs_
    ys = (ys.reshape(bsz, t, GROUP_WIDTH) * jax.nn.silu(sz)).reshape(bsz, t, S_GROUPS, GROUP_WIDTH // S_GROUPS)
    out_s = head_rms(ys).reshape(bsz, t, GROUP_WIDTH) * s_norm_w.astype(f32)

    lh = block_len(t, H_BLOCK)
    q = hq.reshape(bsz, t, H_HEADS, H_KD) * (H_KD ** -0.5)
    k = ((1.0 - lb) * jax.nn.sigmoid(-hf)).reshape(bsz, t, H_HEADS, H_KD)
    lf = jnp.log(lb + (1.0 - lb) * jax.nn.sigmoid(hf)).reshape(bsz, t, H_HEADS, H_KD)
    v = hi.reshape(bsz, t, H_HEADS, H_HD)
    h_s, ho = lax.scan(hgrn_block, h_s, (to_blocks(q, lh), to_blocks(k, lh), to_blocks(v, lh), to_blocks(lf, lh)))
    ho = head_rms(from_blocks(ho)) * h_norm_w.astype(f32).reshape(H_HEADS, H_HD)
    out_h = ho.reshape(bsz, t, GROUP_WIDTH) * jax.nn.silu(hg)

    mix = jnp.concatenate([out_m, out_r, out_s, out_h], axis=-1).astype(x.dtype)
    y = x + jnp.einsum("btc,cd->btd", mix, w_out)
    return y, (m_c, m_n, m_m, r_s, s_h, s_buf, h_s)


def trunk(x, pos, layer_states, lbs, norm_w, w_in, mlstm_gate_b, mlstm_norm_w, ssd_conv_w, ssd_conv_b,
          ssd_dt_bias, ssd_a_log, ssd_d, ssd_norm_w, hgrn_norm_w, w_out, final_norm_w):
    new = []
    for l in range(DEPTH):
        x, st = mixer_layer(x, pos, layer_states[l], norm_w[l], w_in[l], mlstm_gate_b[l], mlstm_norm_w[l],
                            ssd_conv_w[l], ssd_conv_b[l], ssd_dt_bias[l], ssd_a_log[l], ssd_d[l],
                            ssd_norm_w[l], lbs[l], hgrn_norm_w[l], w_out[l])
        new.append(st)
    stacked = [jnp.stack([st[i] for st in new], axis=0) for i in range(7)]
    return rmsnorm(x, final_norm_w), stacked


def setup_inputs(seed: int = 0) -> dict:
    key = jax.random.key(seed)
    ks = jax.random.split(key, 24)
    f32 = jnp.float32

    def nrm(k, shape, s=1.0):
        return s * jax.random.normal(k, shape, f32)

    dt = jnp.exp(jax.random.uniform(ks[14], (DEPTH, S_HEADS), f32, math.log(1e-3), math.log(1e-1)))
    gate_b = jnp.concatenate([nrm(ks[11], (DEPTH, M_HEADS), 0.1),
                              jnp.linspace(3.0, 6.0, M_HEADS, dtype=f32)[None] + nrm(ks[12], (DEPTH, M_HEADS), 0.1)],
                             axis=1)
    return {
        "x_prompt": nrm(ks[0], (BATCH, SEQ, D_MODEL)),
        "x_sample": nrm(ks[1], (DEC_BATCH, DEC_SEQ, D_MODEL)),
        "state_mlstm_c": nrm(ks[2], (DEPTH, DEC_BATCH, M_HEADS, M_QK, M_HD)),
        "state_mlstm_n": nrm(ks[3], (DEPTH, DEC_BATCH, M_HEADS, M_QK)),
        "state_mlstm_m": nrm(ks[4], (DEPTH, DEC_BATCH, M_HEADS)),
        "state_ret": nrm(ks[5], (DEPTH, DEC_BATCH, R_HEADS, R_QK, R_HD)),
        "state_ssd": nrm(ks[6], (DEPTH, DEC_BATCH, S_HEADS, S_STATE, S_HD), 0.5),
        "cache_ssd_conv": nrm(ks[7], (DEPTH, DEC_BATCH, CONV_W - 1, S_CONV_DIM)),
        "state_hgrn": nrm(ks[8], (DEPTH, DEC_BATCH, H_HEADS, H_KD, H_HD)),
        "norm_w": 1.0 + nrm(ks[9], (DEPTH, D_MODEL), 0.1),
        "w_in": nrm(ks[10], (DEPTH, D_MODEL, IN_COLS), D_MODEL ** -0.5),
        "mlstm_gate_b": gate_b,
        "mlstm_norm_w": 1.0 + nrm(ks[13], (DEPTH, GROUP_WIDTH), 0.1),
        "ssd_conv_w": nrm(ks[15], (DEPTH, CONV_W, S_CONV_DIM), CONV_W ** -0.5),
        "ssd_conv_b": nrm(ks[16], (DEPTH, S_CONV_DIM), 0.01),
        "ssd_dt_bias": dt + jnp.log(-jnp.expm1(-dt)),
        "ssd_a_log": jnp.log(jax.random.uniform(ks[17], (DEPTH, S_HEADS), f32, 1.0, 16.0)),
        "ssd_d": 1.0 + nrm(ks[18], (DEPTH, S_HEADS), 0.1),
        "ssd_norm_w": 1.0 + nrm(ks[19], (DEPTH, GROUP_WIDTH), 0.1),
        "hgrn_lower_bounds": nrm(ks[20], (DEPTH, H_KW), 0.1),
        "hgrn_norm_w": 1.0 + nrm(ks[21], (DEPTH, GROUP_WIDTH), 0.1),
        "w_out": nrm(ks[22], (DEPTH, MIX_WIDTH, D_MODEL), MIX_WIDTH ** -0.5),
        "final_norm_w": 1.0 + nrm(ks[23], (D_MODEL,), 0.1),
    }


def reference(x_prompt, x_sample, state_mlstm_c, state_mlstm_n, state_mlstm_m, state_ret, state_ssd,
              cache_ssd_conv, state_hgrn, norm_w, w_in, mlstm_gate_b, mlstm_norm_w, ssd_conv_w, ssd_conv_b,
              ssd_dt_bias, ssd_a_log, ssd_d, ssd_norm_w, hgrn_lower_bounds, hgrn_norm_w, w_out, final_norm_w):
    f32 = jnp.float32
    p = jax.nn.softmax(hgrn_lower_bounds.astype(f32), axis=0)
    lbs = jnp.cumsum(p, axis=0) - p[0]
    weights = (norm_w, w_in, mlstm_gate_b, mlstm_norm_w, ssd_conv_w, ssd_conv_b, ssd_dt_bias, ssd_a_log,
               ssd_d, ssd_norm_w, hgrn_norm_w, w_out, final_norm_w)

    bp, tp = x_prompt.shape[0], x_prompt.shape[1]
    empty = (jnp.zeros((bp, M_HEADS, M_QK, M_HD), f32), jnp.zeros((bp, M_HEADS, M_QK), f32),
             jnp.full((bp, M_HEADS), NEG_INIT, f32), jnp.zeros((bp, R_HEADS, R_QK, R_HD), f32),
             jnp.zeros((bp, S_HEADS, S_STATE, S_HD), f32), jnp.zeros((bp, CONV_W - 1, S_CONV_DIM), f32),
             jnp.zeros((bp, H_HEADS, H_KD, H_HD), f32))
    y_prompt, p_st = trunk(x_prompt, jnp.arange(tp), [empty] * DEPTH, lbs, *weights)

    ts = x_sample.shape[1]
    carried = [(state_mlstm_c[l], state_mlstm_n[l], state_mlstm_m[l], state_ret[l], state_ssd[l],
                cache_ssd_conv[l], state_hgrn[l]) for l in range(DEPTH)]
    y_sample, s_st = trunk(x_sample, PAST_LEN + jnp.arange(ts), carried, lbs, *weights)

    p_mlstm_c, p_mlstm_n, p_mlstm_m, p_ret, p_ssd, p_conv, p_hgrn = p_st
    s_mlstm_c, s_mlstm_n, s_mlstm_m, s_ret, s_ssd, s_conv, s_hgrn = s_st
    return (y_prompt, y_sample, p_mlstm_c, p_mlstm_n, p_mlstm_m, p_ret, p_ssd, p_conv, p_hgrn,
            s_mlstm_c, s_mlstm_n, s_mlstm_m, s_ret, s_ssd, s_conv, s_hgrn)
```

```python
import functools

import numpy as np
import jax
import jax.numpy as jnp
from jax import lax
from jax.experimental import pallas as pl
from jax.experimental.pallas import tpu as pltpu

F32 = jnp.float32
BF16 = jnp.bfloat16

D_MODEL = 1024
GROUP_WIDTH = 512
MIX_WIDTH = 4 * GROUP_WIDTH
EPS = 1e-6
NEG_INIT = -1e30
NEG_BIG = -1e30
CHUNK = 64
PAST_LEN = 1024
ROPE_BASE = 10000.0
HEADS = 4
QK = 64
HD = 128
S_HEADS = 8
S_HD = 64
S_GROUPS = 2
S_STATE = 128
S_GROUP_LANES = (S_HEADS // S_GROUPS) * S_HD
CONV_W = 4
S_CONV_DIM = 1024
H_BLOCK = 16
LANES = 128
SUBLANES = 8
VMEM_LIMIT_BYTES = 56 * 1024 * 1024

IN_SIZES = (256, 256, 512, 512, 512, 8, 256, 256, 512, 512, 512, 1024, 8, 512, 512, 512, 512)
SEC_NAMES = ("mq", "mk", "mv", "mo", "mz", "mif", "rq", "rk", "rv", "rg", "sz", "sxbc", "sdt",
             "hq", "hf", "hi", "hg")


def _padded(width):
    return -(-width // LANES) * LANES


SEC = {}
_off = 0
for _n, _w in zip(SEC_NAMES, IN_SIZES):
    SEC[_n] = (_off, _padded(_w))
    _off += _padded(_w)
PROJ_COLS = _off


def _dot(a, b):
    return jnp.dot(a, b, preferred_element_type=F32)


def _dot_nt(a, b):
    return lax.dot_general(a, b, (((1,), (1,)), ((), ())), preferred_element_type=F32)


def _dot_tn(a, b):
    return lax.dot_general(a, b, (((0,), (0,)), ((), ())), preferred_element_type=F32)


def _sigmoid(x):
    return jax.nn.sigmoid(x)


def _silu(x):
    return x * jax.nn.sigmoid(x)


def _softplus(x):
    return jnp.maximum(x, 0.0) + jnp.log1p(jnp.exp(-jnp.abs(x)))


def _log_sigmoid(x):
    return jnp.minimum(x, 0.0) - jnp.log1p(jnp.exp(-jnp.abs(x)))


def _cumsum_rows(x, n):
    rid = lax.broadcasted_iota(jnp.int32, x.shape, 0)
    s = 1
    while s < n:
        x = x + jnp.where(rid >= s, pltpu.roll(x, s, 0), 0.0)
        s *= 2
    return x


def _cummax_rows(x, n):
    rid = lax.broadcasted_iota(jnp.int32, x.shape, 0)
    s = 1
    while s < n:
        x = jnp.maximum(x, jnp.where(rid >= s, pltpu.roll(x, s, 0), x))
        s *= 2
    return x


def _col_to_row(col, eye):
    return jnp.sum(jnp.where(eye, col, 0.0), axis=0, keepdims=True)


def _expand_heads64(c):
    lane = lax.broadcasted_iota(jnp.int32, (c.shape[0], LANES), 1)
    parts = [jnp.where(lane < S_HD, c[:, 2 * j:2 * j + 1], c[:, 2 * j + 1:2 * j + 2])
             for j in range(S_HEADS // 2)]
    return jnp.concatenate(parts, axis=1)


def _rms(h, width):
    return h * lax.rsqrt(jnp.sum(h * h, axis=-1, keepdims=True) * (1.0 / width) + EPS)


def _layer_kernel(x_ref, cos_ref, sin_ref, mc_i, mn_i, mm_i, rs_i, sh_i, sb_i, hs_i,
                  nw_ref, win_ref, gb_ref, mnw_ref, cw_ref, cb_ref, dtb_ref, alog_ref, dsk_ref,
                  snw_ref, lb_ref, hnw_ref, wout_ref, fnw_ref,
                  y_ref, mc_o, mn_o, mm_o, rs_o, sh_o, sb_o, hs_o,
                  proj_ref, mix_ref, ext_ref, *, layer, last, carry, tm, lc, sb, nt):
    L = lc
    nch = tm // lc
    t_id = pl.program_id(1)

    @pl.when(t_id == 0)
    def _init():
        mc_o[...] = mc_i[...]
        mn_o[...] = mn_i[...]
        mm_o[...] = mm_i[...]
        rs_o[...] = rs_i[...]
        sh_o[...] = sh_i[...]
        sb_o[...] = sb_i[...]
        for s in range(sb):
            for h in range(HEADS):
                hs_o[s, h] = hs_i[s, h].T

    x = x_ref[0]
    hn = (x * lax.rsqrt(jnp.sum(x * x, axis=-1, keepdims=True) * (1.0 / D_MODEL) + EPS)
          * nw_ref[...]).astype(BF16)
    proj_ref[...] = _dot(hn, win_ref[...])

    ri = lax.broadcasted_iota(jnp.int32, (L, L), 0)
    ci = lax.broadcasted_iota(jnp.int32, (L, L), 1)
    tril = ri >= ci
    eye = ri == ci
    dti = (ri - ci).astype(F32)
    rowf = lax.broadcasted_iota(jnp.int32, (L, 1), 0).astype(F32)
    ret_dec, ret_rowdec, ret_kdec, ret_cdec = [], [], [], []
    for h in range(HEADS):
        lg = float(np.log1p(-(np.float32(2.0) ** np.float32(-5.0 - h))).astype(np.float32))
        ret_dec.append(jnp.exp(jnp.where(tril, dti * lg, NEG_BIG)))
        ret_rowdec.append(jnp.exp((rowf + 1.0) * lg))
        ret_kdec.append(jnp.exp((L - 1.0 - rowf) * lg))
        ret_cdec.append(float(np.exp(np.float32(L) * np.float32(lg))))
    lane256 = lax.broadcasted_iota(jnp.int32, (L, S_GROUP_LANES), 1)
    lane_in_seg = lane256 % QK
    head_of_lane = lane256 // S_HD
    row16 = lax.broadcasted_iota(jnp.int32, (H_BLOCK, 1), 0)

    lbp = lb_ref[...]
    lbe = jnp.exp(lbp - jnp.max(lbp, axis=0, keepdims=True))
    lbs = lbe / jnp.sum(lbe, axis=0, keepdims=True)
    lbc = lbs[0:1]
    for i in range(1, layer + 1):
        lbc = lbc + lbs[i:i + 1]
    lbr = lbc - lbs[0:1]

    def rot(v, cosf, sinf):
        sw = jnp.where(lane_in_seg < QK // 2,
                       pltpu.roll(v, S_GROUP_LANES - QK // 2, 1), pltpu.roll(v, QK // 2, 1))
        return v * cosf + sw * sinf

    def chunk(c, carry_val):
        r0 = pl.multiple_of(c * L, L)
        rows = pl.ds(r0, L)
        s = 0 if carry else c

        def sec(name):
            off, w = SEC[name]
            return proj_ref[rows, off:off + w]

        g = sec("mif") + gb_ref[...]
        ig = g
        lf = _log_sigmoid(pltpu.roll(g, LANES - HEADS, 1))
        b = _cumsum_rows(lf, L)
        a = ig - b
        m_prev = mm_o[s]
        big_a = jnp.maximum(m_prev, _cummax_rows(a, L))
        m_t = b + big_a
        wi = jnp.exp(m_prev - big_a)
        em = jnp.exp(-m_t)
        a_last = big_a[L - 1:L]
        wk = jnp.exp(a - a_last)
        wc = jnp.exp(m_prev - a_last)
        mm_o[s] = m_t[L - 1:L]
        q_all = sec("mq") * (QK ** -0.5)
        k_all = sec("mk")
        v_all = sec("mv")
        o_all = sec("mo")
        z_all = sec("mz")
        for h in range(HEADS):
            ks = slice(h * QK, (h + 1) * QK)
            vs = slice(h * HD, (h + 1) * HD)
            qh, kh = q_all[:, ks], k_all[:, ks]
            qb, kb, vb = qh.astype(BF16), kh.astype(BF16), v_all[:, vs].astype(BF16)
            a_row = _col_to_row(a[:, h:h + 1], eye)
            e = jnp.exp(jnp.where(tril, a_row - big_a[:, h:h + 1], NEG_BIG))
            sc = _dot_nt(qb, kb) * e
            c_prev = mc_o[s, h]
            n_prev = mn_o[s, pl.ds(h, 1), :]
            wi_c = wi[:, h:h + 1]
            num = _dot(sc.astype(BF16), vb) + wi_c * _dot(qb, c_prev.astype(BF16))
            den = (jnp.sum(sc, axis=-1, keepdims=True)
                   + wi_c * jnp.sum(qh * n_prev, axis=-1, keepdims=True))
            hh = num / jnp.maximum(jnp.abs(den), em[:, h:h + 1])
            kw = kh * wk[:, h:h + 1]
            wc_h = wc[:, h:h + 1]
            mc_o[s, h] = wc_h * c_prev + _dot_tn(kw.astype(BF16), vb)
            mn_o[s, pl.ds(h, 1), :] = wc_h * n_prev + jnp.sum(kw, axis=0, keepdims=True)
            out = (_rms(hh, HD) * mnw_ref[:, vs]) * _sigmoid(o_all[:, vs]) * _silu(z_all[:, vs])
            mix_ref[rows, h * HD:(h + 1) * HD] = out.astype(BF16)

        cosf = cos_ref[rows, :]
        sinf = sin_ref[rows, :]
        rq = rot(sec("rq"), cosf, sinf) * (QK ** -0.5)
        rk = rot(sec("rk"), cosf, sinf)
        rv = sec("rv")
        rg = sec("rg")
        for h in range(HEADS):
            ks = slice(h * QK, (h + 1) * QK)
            vs = slice(h * HD, (h + 1) * HD)
            kh = rk[:, ks]
            qb, kb, vb = rq[:, ks].astype(BF16), kh.astype(BF16), rv[:, vs].astype(BF16)
            sc = _dot_nt(qb, kb) * ret_dec[h]
            s_prev = rs_o[s, h]
            o = _dot(sc.astype(BF16), vb) + ret_rowdec[h] * _dot(qb, s_prev.astype(BF16))
            rs_o[s, h] = ret_cdec[h] * s_prev + _dot_tn((kh * ret_kdec[h]).astype(BF16), vb)
            cen = o - jnp.sum(o, axis=-1, keepdims=True) * (1.0 / HD)
            out = _rms(cen, HD) * _silu(rg[:, vs])
            mix_ref[rows, GROUP_WIDTH + h * HD:GROUP_WIDTH + (h + 1) * HD] = out.astype(BF16)

        ext_ref[SUBLANES - (CONV_W - 1):SUBLANES, :] = sb_o[s]
        ext_ref[SUBLANES:SUBLANES + L, :] = sec("sxbc")
        acc = cb_ref[...] + cw_ref[CONV_W - 1:CONV_W, :] * ext_ref[SUBLANES:SUBLANES + L, :]
        for j in range(CONV_W - 1):
            st = SUBLANES - (CONV_W - 1) + j
            acc = acc + cw_ref[j:j + 1, :] * ext_ref[st:st + L, :]
        sb_o[s] = ext_ref[SUBLANES + L - (CONV_W - 1):SUBLANES + L, :]
        xbc = _silu(acc)
        xs = xbc[:, :GROUP_WIDTH]
        bs = xbc[:, GROUP_WIDTH:GROUP_WIDTH + S_GROUPS * S_STATE]
        cs = xbc[:, GROUP_WIDTH + S_GROUPS * S_STATE:]
        dtc = _softplus(sec("sdt") + dtb_ref[...])
        la = dtc * (-jnp.exp(alog_ref[...]))
        bcum = _cumsum_rows(la, L)
        b_last = bcum[L - 1:L]
        xv = xs * _expand_heads64(dtc)
        xw = xv * _expand_heads64(jnp.exp(b_last - bcum))
        eb_e = _expand_heads64(jnp.exp(bcum))
        ebl_e = _expand_heads64(jnp.exp(b_last))
        ys = []
        for gi in range(S_GROUPS):
            gl = slice(gi * S_GROUP_LANES, (gi + 1) * S_GROUP_LANES)
            cq = cs[:, gi * S_STATE:(gi + 1) * S_STATE].astype(BF16)
            bk = bs[:, gi * S_STATE:(gi + 1) * S_STATE].astype(BF16)
            cbm = _dot_nt(cq, bk)
            h_prev = sh_o[s, gi]
            y_g = eb_e[:, gl] * _dot(cq, h_prev.astype(BF16))
            xv_g = xv[:, gl]
            for hh in range(S_HEADS // S_GROUPS):
                h = gi * (S_HEADS // S_GROUPS) + hh
                b_col = bcum[:, h:h + 1]
                dec = jnp.exp(jnp.where(tril, b_col - _col_to_row(b_col, eye), NEG_BIG))
                xm = jnp.where(head_of_lane == hh, xv_g, 0.0).astype(BF16)
                y_g = y_g + _dot((cbm * dec).astype(BF16), xm)
            sh_o[s, gi] = ebl_e[:, gl] * h_prev + _dot_tn(bk, xw[:, gl].astype(BF16))
            ys.append(y_g)
        y_s = jnp.concatenate(ys, axis=1) + dsk_ref[...] * xs
        y_s = y_s * _silu(sec("sz"))
        for gi in range(S_GROUPS):
            gl = slice(gi * S_GROUP_LANES, (gi + 1) * S_GROUP_LANES)
            out = _rms(y_s[:, gl], S_GROUP_LANES) * snw_ref[:, gl]
            mix_ref[rows, 2 * GROUP_WIDTH + gi * S_GROUP_LANES:
                    2 * GROUP_WIDTH + (gi + 1) * S_GROUP_LANES] = out.astype(BF16)

        hq = sec("hq") * (HD ** -0.5)
        hf = sec("hf")
        hv = sec("hi")
        hg = sec("hg")
        kk = (1.0 - lbr) * _sigmoid(-hf)
        lfh = jnp.log(lbr + (1.0 - lbr) * _sigmoid(hf))
        for blk in range(L // H_BLOCK):
            bsl = slice(blk * H_BLOCK, (blk + 1) * H_BLOCK)
            b16 = _cumsum_rows(lfh[bsl], H_BLOCK)
            bl = b16[H_BLOCK - 1:H_BLOCK]
            qe = hq[bsl] * jnp.exp(b16)
            kend = kk[bsl] * jnp.exp(bl - b16)
            ebl = jnp.exp(bl)
            for h in range(HEADS):
                vs = slice(h * HD, (h + 1) * HD)
                st_prev = hs_o[s, h]
                o = _dot_nt(qe[:, vs].astype(BF16), st_prev.astype(BF16))
                bh, qh, kh, vh = b16[:, vs], hq[bsl, vs], kk[bsl, vs], hv[bsl, vs]
                for sp in range(H_BLOCK):
                    w = jnp.exp(bh - bh[sp:sp + 1]) * qh * kh[sp:sp + 1]
                    scol = jnp.where(row16 >= sp, jnp.sum(w, axis=-1, keepdims=True), 0.0)
                    o = o + scol * vh[sp:sp + 1]
                hs_o[s, h] = st_prev * ebl[:, vs] + _dot_tn(vh.astype(BF16), kend[:, vs].astype(BF16))
                out = _rms(o, HD) * hnw_ref[:, vs] * _silu(hg[bsl, vs])
                mix_ref[pl.ds(r0 + blk * H_BLOCK, H_BLOCK),
                        3 * GROUP_WIDTH + h * HD:3 * GROUP_WIDTH + (h + 1) * HD] = out.astype(BF16)
        return carry_val

    lax.fori_loop(0, nch, chunk, 0)

    y = _dot(mix_ref[...], wout_ref[...]) + x
    if last:
        y = _rms(y, D_MODEL) * fnw_ref[...]
    y_ref[0] = y

    @pl.when(t_id == nt - 1)
    def _fin():
        for s in range(sb):
            for h in range(HEADS):
                hs_o[s, h] = hs_o[s, h].T


def _const_spec(shape):
    nd = len(shape)
    return pl.BlockSpec(shape, lambda b, t, _nd=nd: (0,) * _nd, pipeline_mode=pl.Buffered(1))


def _layer_call(x, cosf, sinf, states, weights, *, layer, last, carry, tm, lc):
    bsz, t_len, _ = x.shape
    nt = t_len // tm
    sb = 1 if carry else states[0].shape[0]
    grid = (bsz, nt)

    def state_spec(arr):
        blk = (sb,) + arr.shape[1:]
        nd = arr.ndim
        return pl.BlockSpec(blk, lambda b, t, _nd=nd: (b,) + (0,) * (_nd - 1))

    in_specs = ([pl.BlockSpec((1, tm, D_MODEL), lambda b, t: (b, t, 0)),
                 pl.BlockSpec((tm, S_GROUP_LANES), lambda b, t: (t, 0)),
                 pl.BlockSpec((tm, S_GROUP_LANES), lambda b, t: (t, 0))]
                + [state_spec(a) for a in states]
                + [_const_spec(w.shape) for w in weights])
    out_shape = ([jax.ShapeDtypeStruct(x.shape, F32)]
                 + [jax.ShapeDtypeStruct(a.shape, F32) for a in states])
    out_specs = ([pl.BlockSpec((1, tm, D_MODEL), lambda b, t: (b, t, 0))]
                 + [state_spec(a) for a in states])
    kern = functools.partial(_layer_kernel, layer=layer, last=last, carry=carry, tm=tm, lc=lc,
                             sb=sb, nt=nt)
    outs = pl.pallas_call(
        kern,
        grid=grid,
        in_specs=in_specs,
        out_specs=out_specs,
        out_shape=out_shape,
        scratch_shapes=[pltpu.VMEM((tm, PROJ_COLS), F32),
                        pltpu.VMEM((tm, MIX_WIDTH), BF16),
                        pltpu.VMEM((lc + SUBLANES, S_CONV_DIM), F32)],
        compiler_params=pltpu.CompilerParams(
            dimension_semantics=("arbitrary", "arbitrary"),
            vmem_limit_bytes=VMEM_LIMIT_BYTES),
        name=f"hybrid_layer{layer}_{'prompt' if carry else 'sample'}",
    )(x, cosf, sinf, *states, *weights)
    return outs[0], outs[1:]


def _pack_w_in(w):
    parts, off = [], 0
    for width in IN_SIZES:
        p = w[:, off:off + width]
        if _padded(width) != width:
            p = jnp.pad(p, ((0, 0), (0, _padded(width) - width)))
        parts.append(p)
        off += width
    return jnp.concatenate(parts, axis=1).astype(BF16)


def _pad_lanes(v):
    return jnp.pad(v.astype(F32), (0, LANES - v.shape[0]))[None, :]


def _rope_tables(pos):
    half = QK // 2
    freqs = ROPE_BASE ** (-jnp.arange(half, dtype=F32) / half)
    ang = pos.astype(F32)[:, None] * freqs[None, :]
    cos, sin = jnp.cos(ang), jnp.sin(ang)
    cosf = jnp.tile(jnp.concatenate([cos, cos], axis=1), (1, HEADS))
    sinf = jnp.tile(jnp.concatenate([-sin, sin], axis=1), (1, HEADS))
    return cosf, sinf


def _ssd_state_to_lanes(h):
    b = h.shape[0]
    hpg = S_HEADS // S_GROUPS
    return h.reshape(b, S_GROUPS, hpg, S_STATE, S_HD).transpose(0, 1, 3, 2, 4).reshape(
        b, S_GROUPS, S_STATE, S_GROUP_LANES)


def _ssd_state_from_lanes(h):
    b = h.shape[0]
    hpg = S_HEADS // S_GROUPS
    return h.reshape(b, S_GROUPS, S_STATE, hpg, S_HD).transpose(0, 1, 3, 2, 4).reshape(
        b, S_HEADS, S_STATE, S_HD)


def _run_group(x, cosf, sinf, init_states, layer_weights, *, carry, tm, lc):
    depth = len(layer_weights)
    per_layer = []
    for l in range(depth):
        x, st = _layer_call(x, cosf, sinf, init_states[l], layer_weights[l], layer=l,
                            last=(l == depth - 1), carry=carry, tm=tm, lc=lc)
        mc, mn, mm, rs, sh, sbuf, hs = st
        per_layer.append((mc, mn, mm[:, 0, :HEADS], rs, _ssd_state_from_lanes(sh), sbuf, hs))
    stacked = [jnp.stack([st[i] for st in per_layer], axis=0) for i in range(7)]
    return x, stacked


def kernel(x_prompt, x_sample, state_mlstm_c, state_mlstm_n, state_mlstm_m, state_ret, state_ssd,
           cache_ssd_conv, state_hgrn, norm_w, w_in, mlstm_gate_b, mlstm_norm_w, ssd_conv_w,
           ssd_conv_b, ssd_dt_bias, ssd_a_log, ssd_d, ssd_norm_w, hgrn_lower_bounds, hgrn_norm_w,
           w_out, final_norm_w):
    depth = w_in.shape[0]
    lb_all = hgrn_lower_bounds.astype(F32)
    layer_weights = []
    for l in range(depth):
        layer_weights.append((
            norm_w[l].astype(F32)[None, :],
            _pack_w_in(w_in[l]),
            _pad_lanes(mlstm_gate_b[l]),
            mlstm_norm_w[l].astype(F32)[None, :],
            ssd_conv_w[l].astype(F32),
            ssd_conv_b[l].astype(F32)[None, :],
            _pad_lanes(ssd_dt_bias[l]),
            _pad_lanes(ssd_a_log[l]),
            jnp.repeat(ssd_d[l].astype(F32), S_HD)[None, :],
            ssd_norm_w[l].astype(F32)[None, :],
            lb_all,
            hgrn_norm_w[l].astype(F32)[None, :],
            w_out[l].astype(BF16),
            final_norm_w.astype(F32)[None, :],
        ))

    bp, tp, _ = x_prompt.shape
    lc_p = CHUNK if tp % CHUNK == 0 else tp
    tm_p = 256 if tp % 256 == 0 else lc_p
    empty = (jnp.zeros((bp, HEADS, QK, HD), F32), jnp.zeros((bp, HEADS, QK), F32),
             jnp.full((bp, 1, LANES), NEG_INIT, F32), jnp.zeros((bp, HEADS, QK, HD), F32),
             jnp.zeros((bp, S_GROUPS, S_STATE, S_GROUP_LANES), F32),
             jnp.zeros((bp, CONV_W - 1, S_CONV_DIM), F32), jnp.zeros((bp, HEADS, HD, HD), F32))
    cos_p, sin_p = _rope_tables(jnp.arange(tp))
    y_prompt, p_st = _run_group(x_prompt, cos_p, sin_p, [empty] * depth, layer_weights,
                                carry=True, tm=tm_p, lc=lc_p)

    bs_, ts, _ = x_sample.shape
    lc_s = CHUNK if ts % CHUNK == 0 else ts
    carried = []
    for l in range(depth):
        mm = jnp.pad(state_mlstm_m[l].astype(F32), ((0, 0), (0, LANES - HEADS)))[:, None, :]
        carried.append((state_mlstm_c[l].astype(F32), state_mlstm_n[l].astype(F32), mm,
                        state_ret[l].astype(F32), _ssd_state_to_lanes(state_ssd[l].astype(F32)),
                        cache_ssd_conv[l].astype(F32), state_hgrn[l].astype(F32)))
    cos_s, sin_s = _rope_tables(jnp.tile(PAST_LEN + jnp.arange(ts), bs_))
    y_sample, s_st = _run_group(x_sample.reshape(1, bs_ * ts, D_MODEL), cos_s, sin_s, carried,
                                layer_weights, carry=False, tm=bs_ * ts, lc=lc_s)
    y_sample = y_sample.reshape(bs_, ts, D_MODEL)

    return (y_prompt, y_sample, *p_st, *s_st)
```

```python
import functools

import numpy as np
import jax
import jax.numpy as jnp
from jax import lax
from jax.experimental import pallas as pl
from jax.experimental.pallas import tpu as pltpu

F32 = jnp.float32
BF16 = jnp.bfloat16

D_MODEL = 1024
GROUP_WIDTH = 512
MIX_WIDTH = 4 * GROUP_WIDTH
EPS = 1e-6
NEG_INIT = -1e30
NEG_BIG = -1e30
CHUNK = 64
PAST_LEN = 1024
ROPE_BASE = 10000.0
HEADS = 4
QK = 64
HD = 128
S_HEADS = 8
S_HD = 64
S_GROUPS = 2
S_STATE = 128
S_GROUP_LANES = (S_HEADS // S_GROUPS) * S_HD
CONV_W = 4
S_CONV_DIM = 1024
H_BLOCK = 16
HGRN_SAFE_EXP = 60.0
LANES = 128
SUBLANES = 8
VMEM_LIMIT_BYTES = 56 * 1024 * 1024

IN_SIZES = (256, 256, 512, 512, 512, 8, 256, 256, 512, 512, 512, 1024, 8, 512, 512, 512, 512)
SEC_NAMES = ("mq", "mk", "mv", "mo", "mz", "mif", "rq", "rk", "rv", "rg", "sz", "sxbc", "sdt",
             "hq", "hf", "hi", "hg")


def _padded(width):
    return -(-width // LANES) * LANES


SEC = {}
_off = 0
for _n, _w in zip(SEC_NAMES, IN_SIZES):
    SEC[_n] = (_off, _padded(_w))
    _off += _padded(_w)
PROJ_COLS = _off


def _dot(a, b):
    return jnp.dot(a, b, preferred_element_type=F32)


def _dot_nt(a, b):
    return lax.dot_general(a, b, (((1,), (1,)), ((), ())), preferred_element_type=F32)


def _dot_tn(a, b):
    return lax.dot_general(a, b, (((0,), (0,)), ((), ())), preferred_element_type=F32)


def _sigmoid(x):
    return 0.5 * jnp.tanh(0.5 * x) + 0.5


def _silu(x):
    return x * _sigmoid(x)


def _softplus(x):
    return jnp.maximum(x, 0.0) + jnp.log1p(jnp.exp(-jnp.abs(x)))


def _log_sigmoid(x):
    return jnp.minimum(x, 0.0) - jnp.log1p(jnp.exp(-jnp.abs(x)))


def _cumsum_rows(x, n):
    rid = lax.broadcasted_iota(jnp.int32, x.shape, 0)
    s = 1
    while s < n:
        x = x + jnp.where(rid >= s, pltpu.roll(x, s, 0), 0.0)
        s *= 2
    return x


def _cumsum_mxu(x, tri):
    hi = x.astype(BF16)
    r1 = x - hi.astype(F32)
    mid = r1.astype(BF16)
    lo = (r1 - mid.astype(F32)).astype(BF16)
    return _dot(tri, hi) + _dot(tri, mid) + _dot(tri, lo)


def _cummax_rows(x, n):
    rid = lax.broadcasted_iota(jnp.int32, x.shape, 0)
    s = 1
    while s < n:
        x = jnp.maximum(x, jnp.where(rid >= s, pltpu.roll(x, s, 0), x))
        s *= 2
    return x


def _col_to_row(col, eye):
    return jnp.sum(jnp.where(eye, col, 0.0), axis=0, keepdims=True)


def _expand_heads64(c):
    lane = lax.broadcasted_iota(jnp.int32, (c.shape[0], LANES), 1)
    parts = [jnp.where(lane < S_HD, c[:, 2 * j:2 * j + 1], c[:, 2 * j + 1:2 * j + 2])
             for j in range(S_HEADS // 2)]
    return jnp.concatenate(parts, axis=1)


def _rms(h, width):
    return h * lax.rsqrt(jnp.sum(h * h, axis=-1, keepdims=True) * (1.0 / width) + EPS)


def _layer_kernel(x_ref, cos_ref, sin_ref, mc_i, mn_i, mm_i, rs_i, sh_i, sb_i, hs_i,
                  nw_ref, win_ref, gb_ref, mnw_ref, cw_ref, cb_ref, dtb_ref, alog_ref, dsk_ref,
                  snw_ref, lb_ref, hnw_ref, wout_ref, fnw_ref,
                  y_ref, mc_o, mn_o, mm_o, rs_o, sh_o, sb_o, hs_o,
                  proj_ref, mix_ref, ext_ref, hnew_ref, *, layer, last, carry, tm, lc, sb, nt):
    L = lc
    nch = tm // lc
    t_id = pl.program_id(1)

    @pl.when(t_id == 0)
    def _init():
        mc_o[...] = mc_i[...]
        mn_o[...] = mn_i[...]
        mm_o[...] = mm_i[...]
        rs_o[...] = rs_i[...]
        sh_o[...] = sh_i[...]
        sb_o[...] = sb_i[...]
        for s in range(sb):
            for h in range(HEADS):
                hs_o[s, h] = hs_i[s, h].T

    x = x_ref[0]
    hn = (x * lax.rsqrt(jnp.sum(x * x, axis=-1, keepdims=True) * (1.0 / D_MODEL) + EPS)
          * nw_ref[...]).astype(BF16)
    proj_ref[...] = _dot(hn, win_ref[...])

    ri = lax.broadcasted_iota(jnp.int32, (L, L), 0)
    ci = lax.broadcasted_iota(jnp.int32, (L, L), 1)
    tril = ri >= ci
    eye = ri == ci
    tri_bf = jnp.where(tril, 1.0, 0.0).astype(BF16)
    dti = (ri - ci).astype(F32)
    rowf = lax.broadcasted_iota(jnp.int32, (L, 1), 0).astype(F32)
    ret_dec, ret_rowdec, ret_kdec, ret_cdec = [], [], [], []
    for h in range(HEADS):
        lg = float(np.log1p(-(np.float32(2.0) ** np.float32(-5.0 - h))).astype(np.float32))
        ret_dec.append(jnp.exp(jnp.where(tril, dti * lg, NEG_BIG)))
        ret_rowdec.append(jnp.exp((rowf + 1.0) * lg))
        ret_kdec.append(jnp.exp((L - 1.0 - rowf) * lg))
        ret_cdec.append(float(np.exp(np.float32(L) * np.float32(lg))))
    lane256 = lax.broadcasted_iota(jnp.int32, (L, S_GROUP_LANES), 1)
    lane_in_seg = lane256 % QK
    head_of_lane = lane256 // S_HD
    row16 = lax.broadcasted_iota(jnp.int32, (H_BLOCK, 1), 0)

    lbp = lb_ref[...]
    lbe = jnp.exp(lbp - jnp.max(lbp, axis=0, keepdims=True))
    lbs = lbe / jnp.sum(lbe, axis=0, keepdims=True)
    lbc = lbs[0:1]
    for i in range(1, layer + 1):
        lbc = lbc + lbs[i:i + 1]
    lbr = lbc - lbs[0:1]

    def rot(v, cosf, sinf):
        sw = jnp.where(lane_in_seg < QK // 2,
                       pltpu.roll(v, S_GROUP_LANES - QK // 2, 1), pltpu.roll(v, QK // 2, 1))
        return v * cosf + sw * sinf

    def chunk(c, carry_val):
        r0 = pl.multiple_of(c * L, L)
        rows = pl.ds(r0, L)
        s = 0 if carry else c

        def sec(name):
            off, w = SEC[name]
            return proj_ref[rows, off:off + w]

        g = sec("mif") + gb_ref[...]
        ig = g
        lf = _log_sigmoid(pltpu.roll(g, LANES - HEADS, 1))
        b = _cumsum_rows(lf, L)
        a = ig - b
        m_prev = mm_o[s]
        big_a = jnp.maximum(m_prev, _cummax_rows(a, L))
        m_t = b + big_a
        wi = jnp.exp(m_prev - big_a)
        em = jnp.exp(-m_t)
        a_last = big_a[L - 1:L]
        wk = jnp.exp(a - a_last)
        wc = jnp.exp(m_prev - a_last)
        mm_o[s] = m_t[L - 1:L]
        q_all = sec("mq") * (QK ** -0.5)
        k_all = sec("mk")
        v_all = sec("mv")
        o_all = sec("mo")
        z_all = sec("mz")
        for h in range(HEADS):
            ks = slice(h * QK, (h + 1) * QK)
            vs = slice(h * HD, (h + 1) * HD)
            qh, kh = q_all[:, ks], k_all[:, ks]
            qb, kb, vb = qh.astype(BF16), kh.astype(BF16), v_all[:, vs].astype(BF16)
            a_row = _col_to_row(a[:, h:h + 1], eye)
            e = jnp.exp(jnp.where(tril, a_row - big_a[:, h:h + 1], NEG_BIG))
            sc = _dot_nt(qb, kb) * e
            c_prev = mc_o[s, h]
            n_prev = mn_o[s, pl.ds(h, 1), :]
            wi_c = wi[:, h:h + 1]
            num = _dot(sc.astype(BF16), vb) + wi_c * _dot(qb, c_prev.astype(BF16))
            den = (jnp.sum(sc, axis=-1, keepdims=True)
                   + wi_c * jnp.sum(qh * n_prev, axis=-1, keepdims=True))
            hh = num / jnp.maximum(jnp.abs(den), em[:, h:h + 1])
            kw = kh * wk[:, h:h + 1]
            wc_h = wc[:, h:h + 1]
            mc_o[s, h] = wc_h * c_prev + _dot_tn(kw.astype(BF16), vb)
            mn_o[s, pl.ds(h, 1), :] = wc_h * n_prev + jnp.sum(kw, axis=0, keepdims=True)
            out = (_rms(hh, HD) * mnw_ref[:, vs]) * _sigmoid(o_all[:, vs]) * _silu(z_all[:, vs])
            mix_ref[rows, h * HD:(h + 1) * HD] = out.astype(BF16)

        cosf = cos_ref[rows, :]
        sinf = sin_ref[rows, :]
        rq = rot(sec("rq"), cosf, sinf) * (QK ** -0.5)
        rk = rot(sec("rk"), cosf, sinf)
        rv = sec("rv")
        rg = sec("rg")
        for h in range(HEADS):
            ks = slice(h * QK, (h + 1) * QK)
            vs = slice(h * HD, (h + 1) * HD)
            kh = rk[:, ks]
            qb, kb, vb = rq[:, ks].astype(BF16), kh.astype(BF16), rv[:, vs].astype(BF16)
            sc = _dot_nt(qb, kb) * ret_dec[h]
            s_prev = rs_o[s, h]
            o = _dot(sc.astype(BF16), vb) + ret_rowdec[h] * _dot(qb, s_prev.astype(BF16))
            rs_o[s, h] = ret_cdec[h] * s_prev + _dot_tn((kh * ret_kdec[h]).astype(BF16), vb)
            cen = o - jnp.sum(o, axis=-1, keepdims=True) * (1.0 / HD)
            out = _rms(cen, HD) * _silu(rg[:, vs])
            mix_ref[rows, GROUP_WIDTH + h * HD:GROUP_WIDTH + (h + 1) * HD] = out.astype(BF16)

        ext_ref[SUBLANES - (CONV_W - 1):SUBLANES, :] = sb_o[s]
        ext_ref[SUBLANES:SUBLANES + L, :] = sec("sxbc")
        acc = cb_ref[...] + cw_ref[CONV_W - 1:CONV_W, :] * ext_ref[SUBLANES:SUBLANES + L, :]
        for j in range(CONV_W - 1):
            st = SUBLANES - (CONV_W - 1) + j
            acc = acc + cw_ref[j:j + 1, :] * ext_ref[st:st + L, :]
        sb_o[s] = ext_ref[SUBLANES + L - (CONV_W - 1):SUBLANES + L, :]
        xbc = _silu(acc)
        xs = xbc[:, :GROUP_WIDTH]
        bs = xbc[:, GROUP_WIDTH:GROUP_WIDTH + S_GROUPS * S_STATE]
        cs = xbc[:, GROUP_WIDTH + S_GROUPS * S_STATE:]
        dtc = _softplus(sec("sdt") + dtb_ref[...])
        la = dtc * (-jnp.exp(alog_ref[...]))
        bcum = _cumsum_rows(la, L)
        b_last = bcum[L - 1:L]
        xv = xs * _expand_heads64(dtc)
        xw = xv * _expand_heads64(jnp.exp(b_last - bcum))
        eb_e = _expand_heads64(jnp.exp(bcum))
        ebl_e = _expand_heads64(jnp.exp(b_last))
        ys = []
        for gi in range(S_GROUPS):
            gl = slice(gi * S_GROUP_LANES, (gi + 1) * S_GROUP_LANES)
            cq = cs[:, gi * S_STATE:(gi + 1) * S_STATE].astype(BF16)
            bk = bs[:, gi * S_STATE:(gi + 1) * S_STATE].astype(BF16)
            cbm = _dot_nt(cq, bk)
            h_prev = sh_o[s, gi]
            y_g = eb_e[:, gl] * _dot(cq, h_prev.astype(BF16))
            xv_g = xv[:, gl]
            for hh in range(S_HEADS // S_GROUPS):
                h = gi * (S_HEADS // S_GROUPS) + hh
                b_col = bcum[:, h:h + 1]
                dec = jnp.exp(jnp.where(tril, b_col - _col_to_row(b_col, eye), NEG_BIG))
                xm = jnp.where(head_of_lane == hh, xv_g, 0.0).astype(BF16)
                y_g = y_g + _dot((cbm * dec).astype(BF16), xm)
            sh_o[s, gi] = ebl_e[:, gl] * h_prev + _dot_tn(bk, xw[:, gl].astype(BF16))
            ys.append(y_g)
        y_s = jnp.concatenate(ys, axis=1) + dsk_ref[...] * xs
        y_s = y_s * _silu(sec("sz"))
        for gi in range(S_GROUPS):
            gl = slice(gi * S_GROUP_LANES, (gi + 1) * S_GROUP_LANES)
            out = _rms(y_s[:, gl], S_GROUP_LANES) * snw_ref[:, gl]
            mix_ref[rows, 2 * GROUP_WIDTH + gi * S_GROUP_LANES:
                    2 * GROUP_WIDTH + (gi + 1) * S_GROUP_LANES] = out.astype(BF16)

        hq = sec("hq") * (HD ** -0.5)
        hv = sec("hi")
        hg = sec("hg")
        hf = sec("hf")
        zf = jnp.exp(-jnp.abs(hf))
        rf = 1.0 / (1.0 + zf)
        zr = zf * rf
        f_pos = hf >= 0.0
        kk = (1.0 - lbr) * jnp.where(f_pos, zr, rf)
        lfh = jnp.log(lbr + (1.0 - lbr) * jnp.where(f_pos, rf, zr))
        bch = _cumsum_mxu(lfh, tri_bf)
        anchor = bch[L // 2 - 1:L // 2]
        blast = bch[L - 1:L]
        dq = bch - anchor
        worst = jnp.max(jnp.max(jnp.abs(dq), axis=1, keepdims=True), axis=0, keepdims=True)
        unsafe = jnp.logical_not(worst[0, 0] <= HGRN_SAFE_EXP)
        qe = hq * jnp.exp(dq)
        ke = kk * jnp.exp(-dq)
        qs = qe * jnp.exp(anchor)
        kend = ke * jnp.exp(blast - anchor)
        ebl = jnp.exp(blast)
        hg_gate = _silu(hg)
        for h in range(HEADS):
            vs = slice(h * HD, (h + 1) * HD)
            vb = hv[:, vs].astype(BF16)
            st_prev = hs_o[s, h]
            sc = jnp.where(tril, _dot_nt(qe[:, vs].astype(BF16), ke[:, vs].astype(BF16)), 0.0)
            o = _dot(sc.astype(BF16), vb) + _dot_nt(qs[:, vs].astype(BF16), st_prev.astype(BF16))
            hnew_ref[h] = st_prev * ebl[:, vs] + _dot_tn(vb, kend[:, vs].astype(BF16))
            out = _rms(o, HD) * hnw_ref[:, vs] * hg_gate[:, vs]
            mix_ref[rows, 3 * GROUP_WIDTH + h * HD:3 * GROUP_WIDTH + (h + 1) * HD] = out.astype(BF16)

        @pl.when(unsafe)
        def _exact_blocks():
            for h in range(HEADS):
                vs = slice(h * HD, (h + 1) * HD)
                st = hs_o[s, h]
                for blk in range(L // H_BLOCK):
                    bsl = slice(blk * H_BLOCK, (blk + 1) * H_BLOCK)
                    bh = _cumsum_rows(lfh[bsl, vs], H_BLOCK)
                    bl = bh[H_BLOCK - 1:H_BLOCK]
                    qh, kh, vh = hq[bsl, vs], kk[bsl, vs], hv[bsl, vs]
                    o = _dot_nt((qh * jnp.exp(bh)).astype(BF16), st.astype(BF16))
                    for sp in range(H_BLOCK):
                        w = jnp.exp(bh - bh[sp:sp + 1]) * qh * kh[sp:sp + 1]
                        scol = jnp.where(row16 >= sp, jnp.sum(w, axis=-1, keepdims=True), 0.0)
                        o = o + scol * vh[sp:sp + 1]
                    st = st * jnp.exp(bl) + _dot_tn(vh.astype(BF16),
                                                    (kh * jnp.exp(bl - bh)).astype(BF16))
                    out = _rms(o, HD) * hnw_ref[:, vs] * hg_gate[bsl, vs]
                    mix_ref[pl.ds(r0 + blk * H_BLOCK, H_BLOCK),
                            3 * GROUP_WIDTH + h * HD:3 * GROUP_WIDTH + (h + 1) * HD] = out.astype(BF16)
                hnew_ref[h] = st

        hs_o[s] = hnew_ref[...]
        return carry_val

    lax.fori_loop(0, nch, chunk, 0)

    y = _dot(mix_ref[...], wout_ref[...]) + x
    if last:
        y = _rms(y, D_MODEL) * fnw_ref[...]
    y_ref[0] = y

    @pl.when(t_id == nt - 1)
    def _fin():
        for s in range(sb):
            for h in range(HEADS):
                hs_o[s, h] = hs_o[s, h].T


def _const_spec(shape):
    nd = len(shape)
    return pl.BlockSpec(shape, lambda b, t, _nd=nd: (0,) * _nd, pipeline_mode=pl.Buffered(1))


def _layer_call(x, cosf, sinf, states, weights, *, layer, last, carry, tm, lc):
    bsz, t_len, _ = x.shape
    nt = t_len // tm
    sb = 1 if carry else states[0].shape[0]
    grid = (bsz, nt)

    def state_spec(arr):
        blk = (sb,) + arr.shape[1:]
        nd = arr.ndim
        return pl.BlockSpec(blk, lambda b, t, _nd=nd: (b,) + (0,) * (_nd - 1))

    in_specs = ([pl.BlockSpec((1, tm, D_MODEL), lambda b, t: (b, t, 0)),
                 pl.BlockSpec((tm, S_GROUP_LANES), lambda b, t: (t, 0)),
                 pl.BlockSpec((tm, S_GROUP_LANES), lambda b, t: (t, 0))]
                + [state_spec(a) for a in states]
                + [_const_spec(w.shape) for w in weights])
    out_shape = ([jax.ShapeDtypeStruct(x.shape, F32)]
                 + [jax.ShapeDtypeStruct(a.shape, F32) for a in states])
    out_specs = ([pl.BlockSpec((1, tm, D_MODEL), lambda b, t: (b, t, 0))]
                 + [state_spec(a) for a in states])
    kern = functools.partial(_layer_kernel, layer=layer, last=last, carry=carry, tm=tm, lc=lc,
                             sb=sb, nt=nt)
    outs = pl.pallas_call(
        kern,
        grid=grid,
        in_specs=in_specs,
        out_specs=out_specs,
        out_shape=out_shape,
        scratch_shapes=[pltpu.VMEM((tm, PROJ_COLS), F32),
                        pltpu.VMEM((tm, MIX_WIDTH), BF16),
                        pltpu.VMEM((lc + SUBLANES, S_CONV_DIM), F32),
                        pltpu.VMEM((HEADS, HD, HD), F32)],
        compiler_params=pltpu.CompilerParams(
            dimension_semantics=("arbitrary", "arbitrary"),
            vmem_limit_bytes=VMEM_LIMIT_BYTES),
        name=f"hybrid_layer{layer}_{'prompt' if carry else 'sample'}",
    )(x, cosf, sinf, *states, *weights)
    return outs[0], outs[1:]


def _pack_w_in(w):
    parts, off = [], 0
    for width in IN_SIZES:
        p = w[:, off:off + width]
        if _padded(width) != width:
            p = jnp.pad(p, ((0, 0), (0, _padded(width) - width)))
        parts.append(p)
        off += width
    return jnp.concatenate(parts, axis=1).astype(BF16)


def _pad_lanes(v):
    return jnp.pad(v.astype(F32), (0, LANES - v.shape[0]))[None, :]


def _rope_tables(pos):
    half = QK // 2
    freqs = ROPE_BASE ** (-jnp.arange(half, dtype=F32) / half)
    ang = pos.astype(F32)[:, None] * freqs[None, :]
    cos, sin = jnp.cos(ang), jnp.sin(ang)
    cosf = jnp.tile(jnp.concatenate([cos, cos], axis=1), (1, HEADS))
    sinf = jnp.tile(jnp.concatenate([-sin, sin], axis=1), (1, HEADS))
    return cosf, sinf


def _ssd_state_to_lanes(h):
    b = h.shape[0]
    hpg = S_HEADS // S_GROUPS
    return h.reshape(b, S_GROUPS, hpg, S_STATE, S_HD).transpose(0, 1, 3, 2, 4).reshape(
        b, S_GROUPS, S_STATE, S_GROUP_LANES)


def _ssd_state_from_lanes(h):
    b = h.shape[0]
    hpg = S_HEADS // S_GROUPS
    return h.reshape(b, S_GROUPS, S_STATE, hpg, S_HD).transpose(0, 1, 3, 2, 4).reshape(
        b, S_HEADS, S_STATE, S_HD)


def _run_group(x, cosf, sinf, init_states, layer_weights, *, carry, tm, lc):
    depth = len(layer_weights)
    per_layer = []
    for l in range(depth):
        x, st = _layer_call(x, cosf, sinf, init_states[l], layer_weights[l], layer=l,
                            last=(l == depth - 1), carry=carry, tm=tm, lc=lc)
        mc, mn, mm, rs, sh, sbuf, hs = st
        per_layer.append((mc, mn, mm[:, 0, :HEADS], rs, _ssd_state_from_lanes(sh), sbuf, hs))
    stacked = [jnp.stack([st[i] for st in per_layer], axis=0) for i in range(7)]
    return x, stacked


def kernel(x_prompt, x_sample, state_mlstm_c, state_mlstm_n, state_mlstm_m, state_ret, state_ssd,
           cache_ssd_conv, state_hgrn, norm_w, w_in, mlstm_gate_b, mlstm_norm_w, ssd_conv_w,
           ssd_conv_b, ssd_dt_bias, ssd_a_log, ssd_d, ssd_norm_w, hgrn_lower_bounds, hgrn_norm_w,
           w_out, final_norm_w):
    depth = w_in.shape[0]
    lb_all = hgrn_lower_bounds.astype(F32)
    layer_weights = []
    for l in range(depth):
        layer_weights.append((
            norm_w[l].astype(F32)[None, :],
            _pack_w_in(w_in[l]),
            _pad_lanes(mlstm_gate_b[l]),
            mlstm_norm_w[l].astype(F32)[None, :],
            ssd_conv_w[l].astype(F32),
            ssd_conv_b[l].astype(F32)[None, :],
            _pad_lanes(ssd_dt_bias[l]),
            _pad_lanes(ssd_a_log[l]),
            jnp.repeat(ssd_d[l].astype(F32), S_HD)[None, :],
            ssd_norm_w[l].astype(F32)[None, :],
            lb_all,
            hgrn_norm_w[l].astype(F32)[None, :],
            w_out[l].astype(BF16),
            final_norm_w.astype(F32)[None, :],
        ))

    bp, tp, _ = x_prompt.shape
    lc_p = CHUNK if tp % CHUNK == 0 else tp
    tm_p = 256 if tp % 256 == 0 else lc_p
    empty = (jnp.zeros((bp, HEADS, QK, HD), F32), jnp.zeros((bp, HEADS, QK), F32),
             jnp.full((bp, 1, LANES), NEG_INIT, F32), jnp.zeros((bp, HEADS, QK, HD), F32),
             jnp.zeros((bp, S_GROUPS, S_STATE, S_GROUP_LANES), F32),
             jnp.zeros((bp, CONV_W - 1, S_CONV_DIM), F32), jnp.zeros((bp, HEADS, HD, HD), F32))
    cos_p, sin_p = _rope_tables(jnp.arange(tp))
    y_prompt, p_st = _run_group(x_prompt, cos_p, sin_p, [empty] * depth, layer_weights,
                                carry=True, tm=tm_p, lc=lc_p)

    bs_, ts, _ = x_sample.shape
    lc_s = CHUNK if ts % CHUNK == 0 else ts
    carried = []
    for l in range(depth):
        mm = jnp.pad(state_mlstm_m[l].astype(F32), ((0, 0), (0, LANES - HEADS)))[:, None, :]
        carried.append((state_mlstm_c[l].astype(F32), state_mlstm_n[l].astype(F32), mm,
                        state_ret[l].astype(F32), _ssd_state_to_lanes(state_ssd[l].astype(F32)),
                        cache_ssd_conv[l].astype(F32), state_hgrn[l].astype(F32)))
    cos_s, sin_s = _rope_tables(jnp.tile(PAST_LEN + jnp.arange(ts), bs_))
    y_sample, s_st = _run_group(x_sample.reshape(1, bs_ * ts, D_MODEL), cos_s, sin_s, carried,
                                layer_weights, carry=False, tm=bs_ * ts, lc=lc_s)
    y_sample = y_sample.reshape(bs_, ts, D_MODEL)

    return (y_prompt, y_sample, *p_st, *s_st)
```

```python
import functools

import numpy as np
import jax
import jax.numpy as jnp
from jax import lax
from jax.experimental import pallas as pl
from jax.experimental.pallas import tpu as pltpu

F32 = jnp.float32
BF16 = jnp.bfloat16

D_MODEL = 1024
GROUP_WIDTH = 512
MIX_WIDTH = 4 * GROUP_WIDTH
EPS = 1e-6
NEG_INIT = -1e30
NEG_BIG = -1e30
CHUNK = 64
PAST_LEN = 1024
ROPE_BASE = 10000.0
HEADS = 4
QK = 64
HD = 128
S_HEADS = 8
S_HD = 64
S_GROUPS = 2
S_STATE = 128
S_GROUP_LANES = (S_HEADS // S_GROUPS) * S_HD
CONV_W = 4
S_CONV_DIM = 1024
H_BLOCK = 16
MXU_COLS = 256
HGRN_SAFE_EXP = 60.0
LANES = 128
SUBLANES = 8
VMEM_LIMIT_BYTES = 56 * 1024 * 1024

IN_SIZES = (256, 256, 512, 512, 512, 8, 256, 256, 512, 512, 512, 1024, 8, 512, 512, 512, 512)
SEC_NAMES = ("mq", "mk", "mv", "mo", "mz", "mif", "rq", "rk", "rv", "rg", "sz", "sxbc", "sdt",
             "hq", "hf", "hi", "hg")


def _padded(width):
    return -(-width // LANES) * LANES


SEC = {}
_off = 0
for _n, _w in zip(SEC_NAMES, IN_SIZES):
    SEC[_n] = (_off, _padded(_w))
    _off += _padded(_w)
PROJ_COLS = _off
PROJ_PAD = -(-PROJ_COLS // (4 * LANES)) * (4 * LANES)


def _dot(a, b):
    return jnp.dot(a, b, preferred_element_type=F32)


def _dot_nt(a, b):
    return lax.dot_general(a, b, (((1,), (1,)), ((), ())), preferred_element_type=F32)


def _dot_tn(a, b):
    return lax.dot_general(a, b, (((0,), (0,)), ((), ())), preferred_element_type=F32)


def _sigmoid(x):
    return 0.5 * jnp.tanh(0.5 * x) + 0.5


def _silu(x):
    return x * _sigmoid(x)


def _softplus(x):
    return jnp.maximum(x, 0.0) + jnp.log1p(jnp.exp(-jnp.abs(x)))


def _log_sigmoid(x):
    return jnp.minimum(x, 0.0) - jnp.log1p(jnp.exp(-jnp.abs(x)))


def _cumsum_rows(x, n):
    rid = lax.broadcasted_iota(jnp.int32, x.shape, 0)
    s = 1
    while s < n:
        x = x + jnp.where(rid >= s, pltpu.roll(x, s, 0), 0.0)
        s *= 2
    return x


def _cumsum_mxu(x, tri):
    hi = x.astype(BF16)
    r1 = x - hi.astype(F32)
    mid = r1.astype(BF16)
    lo = (r1 - mid.astype(F32)).astype(BF16)
    return _dot(tri, hi) + _dot(tri, mid) + _dot(tri, lo)


def _cummax_rows(x, n):
    rid = lax.broadcasted_iota(jnp.int32, x.shape, 0)
    s = 1
    while s < n:
        x = jnp.maximum(x, jnp.where(rid >= s, pltpu.roll(x, s, 0), x))
        s *= 2
    return x


def _col_to_row(col, eye):
    return jnp.sum(jnp.where(eye, col, 0.0), axis=0, keepdims=True)


def _expand_heads64(c):
    lane = lax.broadcasted_iota(jnp.int32, (c.shape[0], LANES), 1)
    parts = [jnp.where(lane < S_HD, c[:, 2 * j:2 * j + 1], c[:, 2 * j + 1:2 * j + 2])
             for j in range(S_HEADS // 2)]
    return jnp.concatenate(parts, axis=1)


def _rms(h, width):
    return h * lax.rsqrt(jnp.sum(h * h, axis=-1, keepdims=True) * (1.0 / width) + EPS)


def _norm_rows(x, w):
    return (x * lax.rsqrt(jnp.sum(x * x, axis=-1, keepdims=True) * (1.0 / D_MODEL) + EPS)
            * w).astype(BF16)


def _layer_kernel(x_ref, xn_ref, cos_ref, sin_ref, mc_i, mn_i, mm_i, rs_i, sh_i, sb_i, hs_i,
                  nw_ref, win_ref, gb_ref, mnw_ref, cw_ref, cb_ref, dtb_ref, alog_ref, dsk_ref,
                  snw_ref, lb_ref, hnw_ref, wout_ref, fnw_ref,
                  y_ref, mc_o, mn_o, mm_o, rs_o, sh_o, sb_o, hs_o,
                  proj_ref, stage_ref, hnn_ref, mix_ref, ext_ref, hnew_ref,
                  *, layer, last, carry, tm, lc, sb, nt, pipelined):
    L = lc
    nch = tm // lc
    t_id = pl.program_id(1)
    if pipelined:
        step = pl.program_id(0) * nt + t_id
    qcols = PROJ_PAD // nch

    @pl.when(t_id == 0)
    def _init():
        mc_o[...] = mc_i[...]
        mn_o[...] = mn_i[...]
        mm_o[...] = mm_i[...]
        rs_o[...] = rs_i[...]
        sh_o[...] = sh_i[...]
        sb_o[...] = sb_i[...]
        for s in range(sb):
            for h in range(HEADS):
                hs_o[s, h] = hs_i[s, h].T

    x = x_ref[0]
    if pipelined:
        @pl.when(step == 0)
        def _first_proj():
            proj_ref[...] = _dot(_norm_rows(x, nw_ref[...]), win_ref[...])
        hnn_ref[...] = _norm_rows(xn_ref[0], nw_ref[...])
    else:
        proj_ref[...] = _dot(_norm_rows(x, nw_ref[...]), win_ref[...])

    ri = lax.broadcasted_iota(jnp.int32, (L, L), 0)
    ci = lax.broadcasted_iota(jnp.int32, (L, L), 1)
    tril = ri >= ci
    eye = ri == ci
    tri_bf = jnp.where(tril, 1.0, 0.0).astype(BF16)
    dti = (ri - ci).astype(F32)
    rowf = lax.broadcasted_iota(jnp.int32, (L, 1), 0).astype(F32)
    ret_dec, ret_rowdec, ret_kdec, ret_cdec = [], [], [], []
    for h in range(HEADS):
        lg = float(np.log1p(-(np.float32(2.0) ** np.float32(-5.0 - h))).astype(np.float32))
        ret_dec.append(jnp.exp(jnp.where(tril, dti * lg, NEG_BIG)))
        ret_rowdec.append(jnp.exp((rowf + 1.0) * lg))
        ret_kdec.append(jnp.exp((L - 1.0 - rowf) * lg))
        ret_cdec.append(float(np.exp(np.float32(L) * np.float32(lg))))
    lane256 = lax.broadcasted_iota(jnp.int32, (L, S_GROUP_LANES), 1)
    lane_in_seg = lane256 % QK
    head_of_lane = lane256 // S_HD
    row16 = lax.broadcasted_iota(jnp.int32, (H_BLOCK, 1), 0)

    lbp = lb_ref[...]
    lbe = jnp.exp(lbp - jnp.max(lbp, axis=0, keepdims=True))
    lbs = lbe / jnp.sum(lbe, axis=0, keepdims=True)
    lbc = lbs[0:1]
    for i in range(1, layer + 1):
        lbc = lbc + lbs[i:i + 1]
    lbr = lbc - lbs[0:1]

    def rot(v, cosf, sinf):
        sw = jnp.where(lane_in_seg < QK // 2,
                       pltpu.roll(v, S_GROUP_LANES - QK // 2, 1), pltpu.roll(v, QK // 2, 1))
        return v * cosf + sw * sinf

    def chunk(c, carry_val):
        r0 = pl.multiple_of(c * L, L)
        rows = pl.ds(r0, L)
        s = 0 if carry else c

        def sec(name):
            off, w = SEC[name]
            return proj_ref[rows, off:off + w]

        pieces = [(o, min(MXU_COLS, qcols - o)) for o in range(0, qcols, MXU_COLS)] if pipelined else []

        def next_proj_piece():
            if pieces:
                o, w = pieces.pop(0)
                cols = pl.ds(pl.multiple_of(c * qcols + o, LANES), w)
                stage_ref[:, cols] = _dot(hnn_ref[...], win_ref[:, cols])

        g = sec("mif") + gb_ref[...]
        ig = g
        lf = _log_sigmoid(pltpu.roll(g, LANES - HEADS, 1))
        b = _cumsum_rows(lf, L)
        a = ig - b
        m_prev = mm_o[s]
        big_a = jnp.maximum(m_prev, _cummax_rows(a, L))
        m_t = b + big_a
        wi = jnp.exp(m_prev - big_a)
        em = jnp.exp(-m_t)
        a_last = big_a[L - 1:L]
        wk = jnp.exp(a - a_last)
        wc = jnp.exp(m_prev - a_last)
        mm_o[s] = m_t[L - 1:L]
        q_all = sec("mq") * (QK ** -0.5)
        k_all = sec("mk")
        v_all = sec("mv")
        o_all = sec("mo")
        z_all = sec("mz")

        def mlstm_head(h):
            ks = slice(h * QK, (h + 1) * QK)
            vs = slice(h * HD, (h + 1) * HD)
            qh, kh = q_all[:, ks], k_all[:, ks]
            qb, kb, vb = qh.astype(BF16), kh.astype(BF16), v_all[:, vs].astype(BF16)
            a_row = _col_to_row(a[:, h:h + 1], eye)
            e = jnp.exp(jnp.where(tril, a_row - big_a[:, h:h + 1], NEG_BIG))
            sc = _dot_nt(qb, kb) * e
            c_prev = mc_o[s, h]
            n_prev = mn_o[s, pl.ds(h, 1), :]
            wi_c = wi[:, h:h + 1]
            num = _dot(sc.astype(BF16), vb) + wi_c * _dot(qb, c_prev.astype(BF16))
            den = (jnp.sum(sc, axis=-1, keepdims=True)
                   + wi_c * jnp.sum(qh * n_prev, axis=-1, keepdims=True))
            hh = num / jnp.maximum(jnp.abs(den), em[:, h:h + 1])
            kw = kh * wk[:, h:h + 1]
            wc_h = wc[:, h:h + 1]
            mc_o[s, h] = wc_h * c_prev + _dot_tn(kw.astype(BF16), vb)
            mn_o[s, pl.ds(h, 1), :] = wc_h * n_prev + jnp.sum(kw, axis=0, keepdims=True)
            out = (_rms(hh, HD) * mnw_ref[:, vs]) * _sigmoid(o_all[:, vs]) * _silu(z_all[:, vs])
            mix_ref[rows, h * HD:(h + 1) * HD] = out.astype(BF16)

        cosf = cos_ref[rows, :]
        sinf = sin_ref[rows, :]
        rq = rot(sec("rq"), cosf, sinf) * (QK ** -0.5)
        rk = rot(sec("rk"), cosf, sinf)
        rv = sec("rv")
        rg = sec("rg")

        def ret_head(h):
            ks = slice(h * QK, (h + 1) * QK)
            vs = slice(h * HD, (h + 1) * HD)
            kh = rk[:, ks]
            qb, kb, vb = rq[:, ks].astype(BF16), kh.astype(BF16), rv[:, vs].astype(BF16)
            sc = _dot_nt(qb, kb) * ret_dec[h]
            s_prev = rs_o[s, h]
            o = _dot(sc.astype(BF16), vb) + ret_rowdec[h] * _dot(qb, s_prev.astype(BF16))
            rs_o[s, h] = ret_cdec[h] * s_prev + _dot_tn((kh * ret_kdec[h]).astype(BF16), vb)
            cen = o - jnp.sum(o, axis=-1, keepdims=True) * (1.0 / HD)
            out = _rms(cen, HD) * _silu(rg[:, vs])
            mix_ref[rows, GROUP_WIDTH + h * HD:GROUP_WIDTH + (h + 1) * HD] = out.astype(BF16)

        for h in range(HEADS):
            mlstm_head(h)
            ret_head(h)
            next_proj_piece()

        ext_ref[SUBLANES - (CONV_W - 1):SUBLANES, :] = sb_o[s]
        ext_ref[SUBLANES:SUBLANES + L, :] = sec("sxbc")
        acc = cb_ref[...] + cw_ref[CONV_W - 1:CONV_W, :] * ext_ref[SUBLANES:SUBLANES + L, :]
        for j in range(CONV_W - 1):
            st = SUBLANES - (CONV_W - 1) + j
            acc = acc + cw_ref[j:j + 1, :] * ext_ref[st:st + L, :]
        sb_o[s] = ext_ref[SUBLANES + L - (CONV_W - 1):SUBLANES + L, :]
        xbc = _silu(acc)
        xs = xbc[:, :GROUP_WIDTH]
        bs = xbc[:, GROUP_WIDTH:GROUP_WIDTH + S_GROUPS * S_STATE]
        cs = xbc[:, GROUP_WIDTH + S_GROUPS * S_STATE:]
        dtc = _softplus(sec("sdt") + dtb_ref[...])
        la = dtc * (-jnp.exp(alog_ref[...]))
        bcum = _cumsum_rows(la, L)
        b_last = bcum[L - 1:L]
        xv = xs * _expand_heads64(dtc)
        xw = xv * _expand_heads64(jnp.exp(b_last - bcum))
        eb_e = _expand_heads64(jnp.exp(bcum))
        ebl_e = _expand_heads64(jnp.exp(b_last))
        ys = []
        for gi in range(S_GROUPS):
            gl = slice(gi * S_GROUP_LANES, (gi + 1) * S_GROUP_LANES)
            cq = cs[:, gi * S_STATE:(gi + 1) * S_STATE].astype(BF16)
            bk = bs[:, gi * S_STATE:(gi + 1) * S_STATE].astype(BF16)
            cbm = _dot_nt(cq, bk)
            h_prev = sh_o[s, gi]
            y_g = eb_e[:, gl] * _dot(cq, h_prev.astype(BF16))
            xv_g = xv[:, gl]
            for hh in range(S_HEADS // S_GROUPS):
                h = gi * (S_HEADS // S_GROUPS) + hh
                b_col = bcum[:, h:h + 1]
                dec = jnp.exp(jnp.where(tril, b_col - _col_to_row(b_col, eye), NEG_BIG))
                xm = jnp.where(head_of_lane == hh, xv_g, 0.0).astype(BF16)
                y_g = y_g + _dot((cbm * dec).astype(BF16), xm)
            sh_o[s, gi] = ebl_e[:, gl] * h_prev + _dot_tn(bk, xw[:, gl].astype(BF16))
            ys.append(y_g)
            next_proj_piece()
        y_s = jnp.concatenate(ys, axis=1) + dsk_ref[...] * xs
        y_s = y_s * _silu(sec("sz"))
        for gi in range(S_GROUPS):
            gl = slice(gi * S_GROUP_LANES, (gi + 1) * S_GROUP_LANES)
            out = _rms(y_s[:, gl], S_GROUP_LANES) * snw_ref[:, gl]
            mix_ref[rows, 2 * GROUP_WIDTH + gi * S_GROUP_LANES:
                    2 * GROUP_WIDTH + (gi + 1) * S_GROUP_LANES] = out.astype(BF16)

        hq = sec("hq") * (HD ** -0.5)
        hv = sec("hi")
        hg = sec("hg")
        hf = sec("hf")
        zf = jnp.exp(-jnp.abs(hf))
        rf = 1.0 / (1.0 + zf)
        zr = zf * rf
        f_pos = hf >= 0.0
        kk = (1.0 - lbr) * jnp.where(f_pos, zr, rf)
        lfh = jnp.log(lbr + (1.0 - lbr) * jnp.where(f_pos, rf, zr))
        bch = _cumsum_mxu(lfh, tri_bf)
        anchor = bch[L // 2 - 1:L // 2]
        blast = bch[L - 1:L]
        dq = bch - anchor
        worst = jnp.max(jnp.max(jnp.abs(dq), axis=1, keepdims=True), axis=0, keepdims=True)
        unsafe = jnp.logical_not(worst[0, 0] <= HGRN_SAFE_EXP)
        qe = hq * jnp.exp(dq)
        ke = kk * jnp.exp(-dq)
        qs = qe * jnp.exp(anchor)
        kend = ke * jnp.exp(blast - anchor)
        ebl = jnp.exp(blast)
        hg_gate = _silu(hg)
        for h in range(HEADS):
            vs = slice(h * HD, (h + 1) * HD)
            vb = hv[:, vs].astype(BF16)
            st_prev = hs_o[s, h]
            sc = jnp.where(tril, _dot_nt(qe[:, vs].astype(BF16), ke[:, vs].astype(BF16)), 0.0)
            o = _dot(sc.astype(BF16), vb) + _dot_nt(qs[:, vs].astype(BF16), st_prev.astype(BF16))
            hnew_ref[h] = st_prev * ebl[:, vs] + _dot_tn(vb, kend[:, vs].astype(BF16))
            out = _rms(o, HD) * hnw_ref[:, vs] * hg_gate[:, vs]
            mix_ref[rows, 3 * GROUP_WIDTH + h * HD:3 * GROUP_WIDTH + (h + 1) * HD] = out.astype(BF16)
            if h % 2 == 1:
                next_proj_piece()
        while pieces:
            next_proj_piece()

        @pl.when(unsafe)
        def _exact_blocks():
            for h in range(HEADS):
                vs = slice(h * HD, (h + 1) * HD)
                st = hs_o[s, h]
                for blk in range(L // H_BLOCK):
                    bsl = slice(blk * H_BLOCK, (blk + 1) * H_BLOCK)
                    bh = _cumsum_rows(lfh[bsl, vs], H_BLOCK)
                    bl = bh[H_BLOCK - 1:H_BLOCK]
                    qh, kh, vh = hq[bsl, vs], kk[bsl, vs], hv[bsl, vs]
                    o = _dot_nt((qh * jnp.exp(bh)).astype(BF16), st.astype(BF16))
                    for sp in range(H_BLOCK):
                        w = jnp.exp(bh - bh[sp:sp + 1]) * qh * kh[sp:sp + 1]
                        scol = jnp.where(row16 >= sp, jnp.sum(w, axis=-1, keepdims=True), 0.0)
                        o = o + scol * vh[sp:sp + 1]
                    st = st * jnp.exp(bl) + _dot_tn(vh.astype(BF16),
                                                    (kh * jnp.exp(bl - bh)).astype(BF16))
                    out = _rms(o, HD) * hnw_ref[:, vs] * hg_gate[bsl, vs]
                    mix_ref[pl.ds(r0 + blk * H_BLOCK, H_BLOCK),
                            3 * GROUP_WIDTH + h * HD:3 * GROUP_WIDTH + (h + 1) * HD] = out.astype(BF16)
                hnew_ref[h] = st

        hs_o[s] = hnew_ref[...]
        return carry_val

    lax.fori_loop(0, nch, chunk, 0)

    if pipelined:
        proj_ref[...] = stage_ref[...]
    y = _dot(mix_ref[...], wout_ref[...]) + x
    if last:
        y = _rms(y, D_MODEL) * fnw_ref[...]
    y_ref[0] = y

    @pl.when(t_id == nt - 1)
    def _fin():
        for s in range(sb):
            for h in range(HEADS):
                hs_o[s, h] = hs_o[s, h].T


def _const_spec(shape):
    nd = len(shape)
    return pl.BlockSpec(shape, lambda b, t, _nd=nd: (0,) * _nd, pipeline_mode=pl.Buffered(1))


def _layer_call(x, cosf, sinf, states, weights, *, layer, last, carry, tm, lc):
    bsz, t_len, _ = x.shape
    nt = t_len // tm
    sb = 1 if carry else states[0].shape[0]
    grid = (bsz, nt)
    n_steps = bsz * nt
    pipelined = n_steps > 1 and PROJ_PAD % ((tm // lc) * LANES) == 0

    def state_spec(arr):
        blk = (sb,) + arr.shape[1:]
        nd = arr.ndim
        return pl.BlockSpec(blk, lambda b, t, _nd=nd: (b,) + (0,) * (_nd - 1))

    def next_tile(b, t):
        nxt = jnp.minimum(b * nt + t + 1, n_steps - 1)
        return (nxt // nt, nxt % nt, 0)

    in_specs = ([pl.BlockSpec((1, tm, D_MODEL), lambda b, t: (b, t, 0)),
                 pl.BlockSpec((1, tm, D_MODEL), next_tile),
                 pl.BlockSpec((tm, S_GROUP_LANES), lambda b, t: (t, 0)),
                 pl.BlockSpec((tm, S_GROUP_LANES), lambda b, t: (t, 0))]
                + [state_spec(a) for a in states]
                + [_const_spec(w.shape) for w in weights])
    out_shape = ([jax.ShapeDtypeStruct(x.shape, F32)]
                 + [jax.ShapeDtypeStruct(a.shape, F32) for a in states])
    out_specs = ([pl.BlockSpec((1, tm, D_MODEL), lambda b, t: (b, t, 0))]
                 + [state_spec(a) for a in states])
    kern = functools.partial(_layer_kernel, layer=layer, last=last, carry=carry, tm=tm, lc=lc,
                             sb=sb, nt=nt, pipelined=pipelined)
    outs = pl.pallas_call(
        kern,
        grid=grid,
        in_specs=in_specs,
        out_specs=out_specs,
        out_shape=out_shape,
        scratch_shapes=[pltpu.VMEM((tm, PROJ_PAD), F32),
                        pltpu.VMEM((tm if pipelined else SUBLANES, PROJ_PAD), F32),
                        pltpu.VMEM((tm, D_MODEL), BF16),
                        pltpu.VMEM((tm, MIX_WIDTH), BF16),
                        pltpu.VMEM((lc + SUBLANES, S_CONV_DIM), F32),
                        pltpu.VMEM((HEADS, HD, HD), F32)],
        compiler_params=pltpu.CompilerParams(
            dimension_semantics=("arbitrary", "arbitrary"),
            vmem_limit_bytes=VMEM_LIMIT_BYTES),
        name=f"hybrid_layer{layer}_{'prompt' if carry else 'sample'}",
    )(x, x, cosf, sinf, *states, *weights)
    return outs[0], outs[1:]


def _pack_w_in(w):
    parts, off = [], 0
    for width in IN_SIZES:
        p = w[:, off:off + width]
        if _padded(width) != width:
            p = jnp.pad(p, ((0, 0), (0, _padded(width) - width)))
        parts.append(p)
        off += width
    parts.append(jnp.zeros((w.shape[0], PROJ_PAD - PROJ_COLS), w.dtype))
    return jnp.concatenate(parts, axis=1).astype(BF16)


def _pad_lanes(v):
    return jnp.pad(v.astype(F32), (0, LANES - v.shape[0]))[None, :]


def _rope_tables(pos):
    half = QK // 2
    freqs = ROPE_BASE ** (-jnp.arange(half, dtype=F32) / half)
    ang = pos.astype(F32)[:, None] * freqs[None, :]
    cos, sin = jnp.cos(ang), jnp.sin(ang)
    cosf = jnp.tile(jnp.concatenate([cos, cos], axis=1), (1, HEADS))
    sinf = jnp.tile(jnp.concatenate([-sin, sin], axis=1), (1, HEADS))
    return cosf, sinf


def _ssd_state_to_lanes(h):
    b = h.shape[0]
    hpg = S_HEADS // S_GROUPS
    return h.reshape(b, S_GROUPS, hpg, S_STATE, S_HD).transpose(0, 1, 3, 2, 4).reshape(
        b, S_GROUPS, S_STATE, S_GROUP_LANES)


def _ssd_state_from_lanes(h):
    b = h.shape[0]
    hpg = S_HEADS // S_GROUPS
    return h.reshape(b, S_GROUPS, S_STATE, hpg, S_HD).transpose(0, 1, 3, 2, 4).reshape(
        b, S_HEADS, S_STATE, S_HD)


def _run_group(x, cosf, sinf, init_states, layer_weights, *, carry, tm, lc):
    depth = len(layer_weights)
    per_layer = []
    for l in range(depth):
        x, st = _layer_call(x, cosf, sinf, init_states[l], layer_weights[l], layer=l,
                            last=(l == depth - 1), carry=carry, tm=tm, lc=lc)
        mc, mn, mm, rs, sh, sbuf, hs = st
        per_layer.append((mc, mn, mm[:, 0, :HEADS], rs, _ssd_state_from_lanes(sh), sbuf, hs))
    stacked = [jnp.stack([st[i] for st in per_layer], axis=0) for i in range(7)]
    return x, stacked


def kernel(x_prompt, x_sample, state_mlstm_c, state_mlstm_n, state_mlstm_m, state_ret, state_ssd,
           cache_ssd_conv, state_hgrn, norm_w, w_in, mlstm_gate_b, mlstm_norm_w, ssd_conv_w,
           ssd_conv_b, ssd_dt_bias, ssd_a_log, ssd_d, ssd_norm_w, hgrn_lower_bounds, hgrn_norm_w,
           w_out, final_norm_w):
    depth = w_in.shape[0]
    lb_all = hgrn_lower_bounds.astype(F32)
    layer_weights = []
    for l in range(depth):
        layer_weights.append((
            norm_w[l].astype(F32)[None, :],
            _pack_w_in(w_in[l]),
            _pad_lanes(mlstm_gate_b[l]),
            mlstm_norm_w[l].astype(F32)[None, :],
            ssd_conv_w[l].astype(F32),
            ssd_conv_b[l].astype(F32)[None, :],
            _pad_lanes(ssd_dt_bias[l]),
            _pad_lanes(ssd_a_log[l]),
            jnp.repeat(ssd_d[l].astype(F32), S_HD)[None, :],
            ssd_norm_w[l].astype(F32)[None, :],
            lb_all,
            hgrn_norm_w[l].astype(F32)[None, :],
            w_out[l].astype(BF16),
            final_norm_w.astype(F32)[None, :],
        ))

    bp, tp, _ = x_prompt.shape
    lc_p = CHUNK if tp % CHUNK == 0 else tp
    tm_p = 256 if tp % 256 == 0 else lc_p
    empty = (jnp.zeros((bp, HEADS, QK, HD), F32), jnp.zeros((bp, HEADS, QK), F32),
             jnp.full((bp, 1, LANES), NEG_INIT, F32), jnp.zeros((bp, HEADS, QK, HD), F32),
             jnp.zeros((bp, S_GROUPS, S_STATE, S_GROUP_LANES), F32),
             jnp.zeros((bp, CONV_W - 1, S_CONV_DIM), F32), jnp.zeros((bp, HEADS, HD, HD), F32))
    cos_p, sin_p = _rope_tables(jnp.arange(tp))
    y_prompt, p_st = _run_group(x_prompt, cos_p, sin_p, [empty] * depth, layer_weights,
                                carry=True, tm=tm_p, lc=lc_p)

    bs_, ts, _ = x_sample.shape
    lc_s = CHUNK if ts % CHUNK == 0 else ts
    carried = []
    for l in range(depth):
        mm = jnp.pad(state_mlstm_m[l].astype(F32), ((0, 0), (0, LANES - HEADS)))[:, None, :]
        carried.append((state_mlstm_c[l].astype(F32), state_mlstm_n[l].astype(F32), mm,
                        state_ret[l].astype(F32), _ssd_state_to_lanes(state_ssd[l].astype(F32)),
                        cache_ssd_conv[l].astype(F32), state_hgrn[l].astype(F32)))
    cos_s, sin_s = _rope_tables(jnp.tile(PAST_LEN + jnp.arange(ts), bs_))
    y_sample, s_st = _run_group(x_sample.reshape(1, bs_ * ts, D_MODEL), cos_s, sin_s, carried,
                                layer_weights, carry=False, tm=bs_ * ts, lc=lc_s)
    y_sample = y_sample.reshape(bs_, ts, D_MODEL)

    return (y_prompt, y_sample, *p_st, *s_st)
```

```python
import functools

import numpy as np
import jax
import jax.numpy as jnp
from jax import lax
from jax.experimental import pallas as pl
from jax.experimental.pallas import tpu as pltpu

F32 = jnp.float32
BF16 = jnp.bfloat16

D_MODEL = 1024
GROUP_WIDTH = 512
MIX_WIDTH = 4 * GROUP_WIDTH
EPS = 1e-6
NEG_INIT = -1e30
NEG_BIG = -1e30
CHUNK = 64
PAST_LEN = 1024
ROPE_BASE = 10000.0
HEADS = 4
QK = 64
HD = 128
S_HEADS = 8
S_HD = 64
S_GROUPS = 2
S_STATE = 128
S_GROUP_LANES = (S_HEADS // S_GROUPS) * S_HD
CONV_W = 4
S_CONV_DIM = 1024
H_BLOCK = 16
MXU_COLS = 256
HGRN_SAFE_EXP = 60.0
LANES = 128
SUBLANES = 8
VMEM_LIMIT_BYTES = 56 * 1024 * 1024

IN_SIZES = (256, 256, 512, 512, 512, 8, 256, 256, 512, 512, 512, 1024, 8, 512, 512, 512, 512)
SEC_NAMES = ("mq", "mk", "mv", "mo", "mz", "mif", "rq", "rk", "rv", "rg", "sz", "sxbc", "sdt",
             "hq", "hf", "hi", "hg")


def _padded(width):
    return -(-width // LANES) * LANES


SEC = {}
_off = 0
for _n, _w in zip(SEC_NAMES, IN_SIZES):
    SEC[_n] = (_off, _padded(_w))
    _off += _padded(_w)
PROJ_COLS = _off
PROJ_PAD = -(-PROJ_COLS // (4 * LANES)) * (4 * LANES)


def _dot(a, b):
    return jnp.dot(a, b, preferred_element_type=F32)


def _dot_nt(a, b):
    return lax.dot_general(a, b, (((1,), (1,)), ((), ())), preferred_element_type=F32)


def _dot_tn(a, b):
    return lax.dot_general(a, b, (((0,), (0,)), ((), ())), preferred_element_type=F32)


def _sigmoid(x):
    return 0.5 * jnp.tanh(0.5 * x) + 0.5


def _silu(x):
    return x * _sigmoid(x)


def _softplus(x):
    return jnp.maximum(x, 0.0) + jnp.log1p(jnp.exp(-jnp.abs(x)))


def _log_sigmoid(x):
    return jnp.minimum(x, 0.0) - jnp.log1p(jnp.exp(-jnp.abs(x)))


def _cumsum_rows(x, n):
    rid = lax.broadcasted_iota(jnp.int32, x.shape, 0)
    s = 1
    while s < n:
        x = x + jnp.where(rid >= s, pltpu.roll(x, s, 0), 0.0)
        s *= 2
    return x


def _cumsum_mxu(x, tri):
    hi = x.astype(BF16)
    r1 = x - hi.astype(F32)
    mid = r1.astype(BF16)
    lo = (r1 - mid.astype(F32)).astype(BF16)
    return _dot(tri, hi) + _dot(tri, mid) + _dot(tri, lo)


def _cummax_rows(x, n):
    rid = lax.broadcasted_iota(jnp.int32, x.shape, 0)
    s = 1
    while s < n:
        x = jnp.maximum(x, jnp.where(rid >= s, pltpu.roll(x, s, 0), x))
        s *= 2
    return x


def _col_to_row(col, eye):
    return jnp.sum(jnp.where(eye, col, 0.0), axis=0, keepdims=True)


def _expand_heads64(c):
    lane = lax.broadcasted_iota(jnp.int32, (c.shape[0], LANES), 1)
    parts = [jnp.where(lane < S_HD, c[:, 2 * j:2 * j + 1], c[:, 2 * j + 1:2 * j + 2])
             for j in range(S_HEADS // 2)]
    return jnp.concatenate(parts, axis=1)


def _rms(h, width):
    return h * lax.rsqrt(jnp.sum(h * h, axis=-1, keepdims=True) * (1.0 / width) + EPS)


def _norm_rows(x, w):
    return (x * lax.rsqrt(jnp.sum(x * x, axis=-1, keepdims=True) * (1.0 / D_MODEL) + EPS)
            * w).astype(BF16)


def _layer_kernel(x_ref, xn_ref, cos_ref, sin_ref, mc_i, mn_i, mm_i, rs_i, sh_i, sb_i, hs_i,
                  nw_ref, win_ref, gb_ref, mnw_ref, cw_ref, cb_ref, dtb_ref, alog_ref, dsk_ref,
                  snw_ref, lb_ref, hnw_ref, wout_ref, fnw_ref,
                  y_ref, mc_o, mn_o, mm_o, rs_o, sh_o, sb_o, hs_o,
                  proj_ref, stage_ref, hnn_ref, mix_ref, ext_ref, hnew_ref,
                  *, layer, last, carry, tm, lc, sb, nt, pipelined):
    L = lc
    nch = tm // lc
    t_id = pl.program_id(1)
    if pipelined:
        step = pl.program_id(0) * nt + t_id
    qcols = PROJ_PAD // nch
    hgrn_col0 = SEC["hq"][0]

    @pl.when(t_id == 0)
    def _init():
        mc_o[...] = mc_i[...]
        mn_o[...] = mn_i[...]
        mm_o[...] = mm_i[...]
        rs_o[...] = rs_i[...]
        sh_o[...] = sh_i[...]
        sb_o[...] = sb_i[...]
        for s in range(sb):
            for h in range(HEADS):
                hs_o[s, h] = hs_i[s, h].T

    x = x_ref[0]
    if pipelined:
        @pl.when(step == 0)
        def _first_proj():
            proj_ref[...] = _dot(_norm_rows(x, nw_ref[...]), win_ref[...])
        hnn_ref[...] = _norm_rows(xn_ref[0], nw_ref[...])
    else:
        proj_ref[...] = _dot(_norm_rows(x, nw_ref[...]), win_ref[...])

    ri = lax.broadcasted_iota(jnp.int32, (L, L), 0)
    ci = lax.broadcasted_iota(jnp.int32, (L, L), 1)
    tril = ri >= ci
    eye = ri == ci
    tri_bf = jnp.where(tril, 1.0, 0.0).astype(BF16)
    dti = (ri - ci).astype(F32)
    rowf = lax.broadcasted_iota(jnp.int32, (L, 1), 0).astype(F32)
    ret_dec, ret_rowdec, ret_kdec, ret_cdec = [], [], [], []
    for h in range(HEADS):
        lg = float(np.log1p(-(np.float32(2.0) ** np.float32(-5.0 - h))).astype(np.float32))
        ret_dec.append(jnp.exp(jnp.where(tril, dti * lg, NEG_BIG)))
        ret_rowdec.append(jnp.exp((rowf + 1.0) * lg))
        ret_kdec.append(jnp.exp((L - 1.0 - rowf) * lg))
        ret_cdec.append(float(np.exp(np.float32(L) * np.float32(lg))))
    lane256 = lax.broadcasted_iota(jnp.int32, (L, S_GROUP_LANES), 1)
    lane_in_seg = lane256 % QK
    head_of_lane = lane256 // S_HD
    row16 = lax.broadcasted_iota(jnp.int32, (H_BLOCK, 1), 0)

    lbp = lb_ref[...]
    lbe = jnp.exp(lbp - jnp.max(lbp, axis=0, keepdims=True))
    lbs = lbe / jnp.sum(lbe, axis=0, keepdims=True)
    lbc = lbs[0:1]
    for i in range(1, layer + 1):
        lbc = lbc + lbs[i:i + 1]
    lbr = lbc - lbs[0:1]

    def rot(v, cosf, sinf):
        sw = jnp.where(lane_in_seg < QK // 2,
                       pltpu.roll(v, S_GROUP_LANES - QK // 2, 1), pltpu.roll(v, QK // 2, 1))
        return v * cosf + sw * sinf

    def hgrn_inputs(rsel):
        def col(name):
            off, w = SEC[name]
            return proj_ref[rsel, off:off + w]
        hf = col("hf")
        zf = jnp.exp(-jnp.abs(hf))
        rf = 1.0 / (1.0 + zf)
        zr = zf * rf
        f_pos = hf >= 0.0
        kk = (1.0 - lbr) * jnp.where(f_pos, zr, rf)
        lfh = jnp.log(lbr + (1.0 - lbr) * jnp.where(f_pos, rf, zr))
        return col("hq") * (HD ** -0.5), kk, lfh, col("hi"), _silu(col("hg"))

    def out_proj_piece(r_lo, r_hi, c_lo):
        def emit():
            cols = slice(c_lo, c_lo + MXU_COLS)
            y_ref[0, r_lo:r_hi, cols] = (_dot(mix_ref[r_lo:r_hi, :], wout_ref[:, cols])
                                         + x_ref[0, r_lo:r_hi, cols])
        return emit

    def finish_rows(r_lo, r_hi):
        if last:
            y_ref[0, r_lo:r_hi, :] = _rms(y_ref[0, r_lo:r_hi, :], D_MODEL) * fnw_ref[...]

    def chunk(c, hgrn_state, extra_mxu_work):
        r0 = c * L
        rows = slice(r0, r0 + L)
        s = 0 if carry else c

        def sec(name):
            off, w = SEC[name]
            return proj_ref[rows, off:off + w]

        def proj_piece(o, w):
            def emit():
                cols = slice(c * qcols + o, c * qcols + o + w)
                stage_ref[:, cols] = _dot(hnn_ref[...], win_ref[:, cols])
            return emit

        pieces = ([proj_piece(o, min(MXU_COLS, qcols - o)) for o in range(0, qcols, MXU_COLS)]
                  if pipelined else [])
        pieces = pieces + list(extra_mxu_work)

        def next_proj_piece():
            if pieces:
                pieces.pop(0)()

        g = sec("mif") + gb_ref[...]
        ig = g
        lf = _log_sigmoid(pltpu.roll(g, LANES - HEADS, 1))
        b = _cumsum_rows(lf, L)
        a = ig - b
        m_prev = mm_o[s]
        big_a = jnp.maximum(m_prev, _cummax_rows(a, L))
        m_t = b + big_a
        wi = jnp.exp(m_prev - big_a)
        em = jnp.exp(-m_t)
        a_last = big_a[L - 1:L]
        wk = jnp.exp(a - a_last)
        wc = jnp.exp(m_prev - a_last)
        mm_o[s] = m_t[L - 1:L]
        q_all = sec("mq") * (QK ** -0.5)
        k_all = sec("mk")
        v_all = sec("mv")
        o_all = sec("mo")
        z_all = sec("mz")

        def mlstm_head(h):
            ks = slice(h * QK, (h + 1) * QK)
            vs = slice(h * HD, (h + 1) * HD)
            qh, kh = q_all[:, ks], k_all[:, ks]
            qb, kb, vb = qh.astype(BF16), kh.astype(BF16), v_all[:, vs].astype(BF16)
            a_row = _col_to_row(a[:, h:h + 1], eye)
            e = jnp.exp(jnp.where(tril, a_row - big_a[:, h:h + 1], NEG_BIG))
            sc = _dot_nt(qb, kb) * e
            c_prev = mc_o[s, h]
            n_prev = mn_o[s, pl.ds(h, 1), :]
            wi_c = wi[:, h:h + 1]
            num = _dot(sc.astype(BF16), vb) + wi_c * _dot(qb, c_prev.astype(BF16))
            den = (jnp.sum(sc, axis=-1, keepdims=True)
                   + wi_c * jnp.sum(qh * n_prev, axis=-1, keepdims=True))
            hh = num / jnp.maximum(jnp.abs(den), em[:, h:h + 1])
            kw = kh * wk[:, h:h + 1]
            wc_h = wc[:, h:h + 1]
            mc_o[s, h] = wc_h * c_prev + _dot_tn(kw.astype(BF16), vb)
            mn_o[s, pl.ds(h, 1), :] = wc_h * n_prev + jnp.sum(kw, axis=0, keepdims=True)
            out = (_rms(hh, HD) * mnw_ref[:, vs]) * _sigmoid(o_all[:, vs]) * _silu(z_all[:, vs])
            mix_ref[rows, h * HD:(h + 1) * HD] = out.astype(BF16)

        cosf = cos_ref[rows, :]
        sinf = sin_ref[rows, :]
        rq = rot(sec("rq"), cosf, sinf) * (QK ** -0.5)
        rk = rot(sec("rk"), cosf, sinf)
        rv = sec("rv")
        rg = sec("rg")

        def ret_head(h):
            ks = slice(h * QK, (h + 1) * QK)
            vs = slice(h * HD, (h + 1) * HD)
            kh = rk[:, ks]
            qb, kb, vb = rq[:, ks].astype(BF16), kh.astype(BF16), rv[:, vs].astype(BF16)
            sc = _dot_nt(qb, kb) * ret_dec[h]
            s_prev = rs_o[s, h]
            o = _dot(sc.astype(BF16), vb) + ret_rowdec[h] * _dot(qb, s_prev.astype(BF16))
            rs_o[s, h] = ret_cdec[h] * s_prev + _dot_tn((kh * ret_kdec[h]).astype(BF16), vb)
            cen = o - jnp.sum(o, axis=-1, keepdims=True) * (1.0 / HD)
            out = _rms(cen, HD) * _silu(rg[:, vs])
            mix_ref[rows, GROUP_WIDTH + h * HD:GROUP_WIDTH + (h + 1) * HD] = out.astype(BF16)

        for h in range(HEADS):
            mlstm_head(h)
            ret_head(h)
            next_proj_piece()

        ext_ref[SUBLANES - (CONV_W - 1):SUBLANES, :] = sb_o[s]
        ext_ref[SUBLANES:SUBLANES + L, :] = sec("sxbc")
        acc = cb_ref[...] + cw_ref[CONV_W - 1:CONV_W, :] * ext_ref[SUBLANES:SUBLANES + L, :]
        for j in range(CONV_W - 1):
            st = SUBLANES - (CONV_W - 1) + j
            acc = acc + cw_ref[j:j + 1, :] * ext_ref[st:st + L, :]
        sb_o[s] = ext_ref[SUBLANES + L - (CONV_W - 1):SUBLANES + L, :]
        xbc = _silu(acc)
        xs = xbc[:, :GROUP_WIDTH]
        bs = xbc[:, GROUP_WIDTH:GROUP_WIDTH + S_GROUPS * S_STATE]
        cs = xbc[:, GROUP_WIDTH + S_GROUPS * S_STATE:]
        dtc = _softplus(sec("sdt") + dtb_ref[...])
        la = dtc * (-jnp.exp(alog_ref[...]))
        bcum = _cumsum_rows(la, L)
        b_last = bcum[L - 1:L]
        xv = xs * _expand_heads64(dtc)
        xw = xv * _expand_heads64(jnp.exp(b_last - bcum))
        eb_e = _expand_heads64(jnp.exp(bcum))
        ebl_e = _expand_heads64(jnp.exp(b_last))
        ys = []
        for gi in range(S_GROUPS):
            gl = slice(gi * S_GROUP_LANES, (gi + 1) * S_GROUP_LANES)
            cq = cs[:, gi * S_STATE:(gi + 1) * S_STATE].astype(BF16)
            bk = bs[:, gi * S_STATE:(gi + 1) * S_STATE].astype(BF16)
            cbm = _dot_nt(cq, bk)
            h_prev = sh_o[s, gi]
            y_g = eb_e[:, gl] * _dot(cq, h_prev.astype(BF16))
            xv_g = xv[:, gl]
            for hh in range(S_HEADS // S_GROUPS):
                h = gi * (S_HEADS // S_GROUPS) + hh
                b_col = bcum[:, h:h + 1]
                dec = jnp.exp(jnp.where(tril, b_col - _col_to_row(b_col, eye), NEG_BIG))
                xm = jnp.where(head_of_lane == hh, xv_g, 0.0).astype(BF16)
                y_g = y_g + _dot((cbm * dec).astype(BF16), xm)
            sh_o[s, gi] = ebl_e[:, gl] * h_prev + _dot_tn(bk, xw[:, gl].astype(BF16))
            ys.append(y_g)
            next_proj_piece()
        y_s = jnp.concatenate(ys, axis=1) + dsk_ref[...] * xs
        y_s = y_s * _silu(sec("sz"))
        for gi in range(S_GROUPS):
            gl = slice(gi * S_GROUP_LANES, (gi + 1) * S_GROUP_LANES)
            out = _rms(y_s[:, gl], S_GROUP_LANES) * snw_ref[:, gl]
            mix_ref[rows, 2 * GROUP_WIDTH + gi * S_GROUP_LANES:
                    2 * GROUP_WIDTH + (gi + 1) * S_GROUP_LANES] = out.astype(BF16)

        hq, kk, lfh, hv, hg_gate = hgrn_inputs(rows)
        bch = _cumsum_mxu(lfh, tri_bf)
        anchor = bch[L // 2 - 1:L // 2]
        blast = bch[L - 1:L]
        dq = bch - anchor
        worst = jnp.max(jnp.max(jnp.abs(dq), axis=1, keepdims=True), axis=0, keepdims=True)
        unsafe = jnp.logical_not(worst[0, 0] <= HGRN_SAFE_EXP)
        qe = hq * jnp.exp(dq)
        ke = kk * jnp.exp(-dq)
        qs = qe * jnp.exp(anchor)
        kend = ke * jnp.exp(blast - anchor)
        ebl = jnp.exp(blast)
        new_state = []
        for h in range(HEADS):
            vs = slice(h * HD, (h + 1) * HD)
            vb = hv[:, vs].astype(BF16)
            st_prev = hgrn_state[h]
            sc = jnp.where(tril, _dot_nt(qe[:, vs].astype(BF16), ke[:, vs].astype(BF16)), 0.0)
            o = _dot(sc.astype(BF16), vb) + _dot_nt(qs[:, vs].astype(BF16), st_prev.astype(BF16))
            new_state.append(st_prev * ebl[:, vs] + _dot_tn(vb, kend[:, vs].astype(BF16)))
            out = _rms(o, HD) * hnw_ref[:, vs] * hg_gate[:, vs]
            mix_ref[rows, 3 * GROUP_WIDTH + h * HD:3 * GROUP_WIDTH + (h + 1) * HD] = out.astype(BF16)
            next_proj_piece()
        while pieces:
            next_proj_piece()
        return new_state, unsafe

    half = tm // 2
    early_out = half % L == 0 and nch >= 2 and D_MODEL % MXU_COLS == 0
    first_half = ([out_proj_piece(0, half, c_lo) for c_lo in range(0, D_MODEL, MXU_COLS)]
                  if early_out else [])
    late_chunks = list(range(half // L, nch)) if early_out else []
    any_unsafe = None
    state = None
    for c in range(nch):
        if state is None or not carry:
            state = [hs_o[0 if carry else c, h] for h in range(HEADS)]
        extra = []
        if c in late_chunks:
            k = late_chunks.index(c)
            per = -(-len(first_half) // len(late_chunks))
            extra = first_half[k * per:(k + 1) * per]
        state, unsafe = chunk(c, state, extra)
        any_unsafe = unsafe if any_unsafe is None else jnp.logical_or(any_unsafe, unsafe)
        if not carry or c == nch - 1:
            for h in range(HEADS):
                hnew_ref[0 if carry else c, h] = state[h]
        if pipelined:
            ready = [(c, q) for q in range(c + 1)] + [(r, c) for r in range(c)]
            for r, q in ready:
                lo, hi = q * qcols, min((q + 1) * qcols, hgrn_col0)
                if lo < hi:
                    proj_ref[r * L:(r + 1) * L, lo:hi] = stage_ref[r * L:(r + 1) * L, lo:hi]

    @pl.when(any_unsafe)
    def _exact_hgrn():
        hnew_ref[...] = hs_o[...]

        def block(i, carry_val):
            r16 = pl.multiple_of(i * H_BLOCK, H_BLOCK)
            brow = pl.ds(r16, H_BLOCK)
            si = 0 if carry else (i if L == H_BLOCK else i // (L // H_BLOCK))
            hq, kk, lfh, hv, hg_gate = hgrn_inputs(brow)
            b16 = _cumsum_rows(lfh, H_BLOCK)
            bl = b16[H_BLOCK - 1:H_BLOCK]
            qe = hq * jnp.exp(b16)
            kend = kk * jnp.exp(bl - b16)
            ebl = jnp.exp(bl)
            for h in range(HEADS):
                vs = slice(h * HD, (h + 1) * HD)
                st = hnew_ref[si, h]
                bh, qh, kh, vh = b16[:, vs], hq[:, vs], kk[:, vs], hv[:, vs]
                o = _dot_nt(qe[:, vs].astype(BF16), st.astype(BF16))
                for sp in range(H_BLOCK):
                    w = jnp.exp(bh - bh[sp:sp + 1]) * qh * kh[sp:sp + 1]
                    scol = jnp.where(row16 >= sp, jnp.sum(w, axis=-1, keepdims=True), 0.0)
                    o = o + scol * vh[sp:sp + 1]
                hnew_ref[si, h] = st * ebl[:, vs] + _dot_tn(vh.astype(BF16), kend[:, vs].astype(BF16))
                out = _rms(o, HD) * hnw_ref[:, vs] * hg_gate[:, vs]
                mix_ref[brow, 3 * GROUP_WIDTH + h * HD:3 * GROUP_WIDTH + (h + 1) * HD] = out.astype(BF16)
            return carry_val

        lax.fori_loop(0, tm // H_BLOCK, block, 0)
        for emit in first_half:
            emit()

    hs_o[...] = hnew_ref[...]

    if pipelined:
        proj_ref[:, hgrn_col0:] = stage_ref[:, hgrn_col0:]
    if early_out:
        finish_rows(0, half)
        for c_lo in range(0, D_MODEL, MXU_COLS):
            out_proj_piece(half, tm, c_lo)()
        finish_rows(half, tm)
    else:
        y = _dot(mix_ref[...], wout_ref[...]) + x
        if last:
            y = _rms(y, D_MODEL) * fnw_ref[...]
        y_ref[0] = y

    @pl.when(t_id == nt - 1)
    def _fin():
        for s in range(sb):
            for h in range(HEADS):
                hs_o[s, h] = hs_o[s, h].T


def _const_spec(shape):
    nd = len(shape)
    return pl.BlockSpec(shape, lambda b, t, _nd=nd: (0,) * _nd, pipeline_mode=pl.Buffered(1))


def _layer_call(x, cosf, sinf, states, weights, *, layer, last, carry, tm, lc):
    bsz, t_len, _ = x.shape
    nt = t_len // tm
    sb = 1 if carry else states[0].shape[0]
    grid = (bsz, nt)
    n_steps = bsz * nt
    pipelined = n_steps > 1 and PROJ_PAD % ((tm // lc) * LANES) == 0

    def state_spec(arr):
        blk = (sb,) + arr.shape[1:]
        nd = arr.ndim
        return pl.BlockSpec(blk, lambda b, t, _nd=nd: (b,) + (0,) * (_nd - 1))

    def next_tile(b, t):
        nxt = jnp.minimum(b * nt + t + 1, n_steps - 1)
        return (nxt // nt, nxt % nt, 0)

    in_specs = ([pl.BlockSpec((1, tm, D_MODEL), lambda b, t: (b, t, 0)),
                 pl.BlockSpec((1, tm, D_MODEL), next_tile),
                 pl.BlockSpec((tm, S_GROUP_LANES), lambda b, t: (t, 0)),
                 pl.BlockSpec((tm, S_GROUP_LANES), lambda b, t: (t, 0))]
                + [state_spec(a) for a in states]
                + [_const_spec(w.shape) for w in weights])
    out_shape = ([jax.ShapeDtypeStruct(x.shape, F32)]
                 + [jax.ShapeDtypeStruct(a.shape, F32) for a in states])
    out_specs = ([pl.BlockSpec((1, tm, D_MODEL), lambda b, t: (b, t, 0))]
                 + [state_spec(a) for a in states])
    kern = functools.partial(_layer_kernel, layer=layer, last=last, carry=carry, tm=tm, lc=lc,
                             sb=sb, nt=nt, pipelined=pipelined)
    outs = pl.pallas_call(
        kern,
        grid=grid,
        in_specs=in_specs,
        out_specs=out_specs,
        out_shape=out_shape,
        scratch_shapes=[pltpu.VMEM((tm, PROJ_PAD), F32),
                        pltpu.VMEM((tm if pipelined else SUBLANES, PROJ_PAD), F32),
                        pltpu.VMEM((tm, D_MODEL), BF16),
                        pltpu.VMEM((tm, MIX_WIDTH), BF16),
                        pltpu.VMEM((lc + SUBLANES, S_CONV_DIM), F32),
                        pltpu.VMEM((sb, HEADS, HD, HD), F32)],
        compiler_params=pltpu.CompilerParams(
            dimension_semantics=("arbitrary", "arbitrary"),
            vmem_limit_bytes=VMEM_LIMIT_BYTES),
        name=f"hybrid_layer{layer}_{'prompt' if carry else 'sample'}",
    )(x, x, cosf, sinf, *states, *weights)
    return outs[0], outs[1:]


def _pack_w_in(w):
    parts, off = [], 0
    for width in IN_SIZES:
        p = w[:, off:off + width]
        if _padded(width) != width:
            p = jnp.pad(p, ((0, 0), (0, _padded(width) - width)))
        parts.append(p)
        off += width
    parts.append(jnp.zeros((w.shape[0], PROJ_PAD - PROJ_COLS), w.dtype))
    return jnp.concatenate(parts, axis=1).astype(BF16)


def _pad_lanes(v):
    return jnp.pad(v.astype(F32), (0, LANES - v.shape[0]))[None, :]


def _rope_tables(pos):
    half = QK // 2
    freqs = ROPE_BASE ** (-jnp.arange(half, dtype=F32) / half)
    ang = pos.astype(F32)[:, None] * freqs[None, :]
    cos, sin = jnp.cos(ang), jnp.sin(ang)
    cosf = jnp.tile(jnp.concatenate([cos, cos], axis=1), (1, HEADS))
    sinf = jnp.tile(jnp.concatenate([-sin, sin], axis=1), (1, HEADS))
    return cosf, sinf


def _ssd_state_to_lanes(h):
    b = h.shape[0]
    hpg = S_HEADS // S_GROUPS
    return h.reshape(b, S_GROUPS, hpg, S_STATE, S_HD).transpose(0, 1, 3, 2, 4).reshape(
        b, S_GROUPS, S_STATE, S_GROUP_LANES)


def _ssd_state_from_lanes(h):
    b = h.shape[0]
    hpg = S_HEADS // S_GROUPS
    return h.reshape(b, S_GROUPS, S_STATE, hpg, S_HD).transpose(0, 1, 3, 2, 4).reshape(
        b, S_HEADS, S_STATE, S_HD)


def _run_group(x, cosf, sinf, init_states, layer_weights, *, carry, tm, lc):
    depth = len(layer_weights)
    per_layer = []
    for l in range(depth):
        x, st = _layer_call(x, cosf, sinf, init_states[l], layer_weights[l], layer=l,
                            last=(l == depth - 1), carry=carry, tm=tm, lc=lc)
        mc, mn, mm, rs, sh, sbuf, hs = st
        per_layer.append((mc, mn, mm[:, 0, :HEADS], rs, _ssd_state_from_lanes(sh), sbuf, hs))
    stacked = [jnp.stack([st[i] for st in per_layer], axis=0) for i in range(7)]
    return x, stacked


def kernel(x_prompt, x_sample, state_mlstm_c, state_mlstm_n, state_mlstm_m, state_ret, state_ssd,
           cache_ssd_conv, state_hgrn, norm_w, w_in, mlstm_gate_b, mlstm_norm_w, ssd_conv_w,
           ssd_conv_b, ssd_dt_bias, ssd_a_log, ssd_d, ssd_norm_w, hgrn_lower_bounds, hgrn_norm_w,
           w_out, final_norm_w):
    depth = w_in.shape[0]
    lb_all = hgrn_lower_bounds.astype(F32)
    layer_weights = []
    for l in range(depth):
        layer_weights.append((
            norm_w[l].astype(F32)[None, :],
            _pack_w_in(w_in[l]),
            _pad_lanes(mlstm_gate_b[l]),
            mlstm_norm_w[l].astype(F32)[None, :],
            ssd_conv_w[l].astype(F32),
            ssd_conv_b[l].astype(F32)[None, :],
            _pad_lanes(ssd_dt_bias[l]),
            _pad_lanes(ssd_a_log[l]),
            jnp.repeat(ssd_d[l].astype(F32), S_HD)[None, :],
            ssd_norm_w[l].astype(F32)[None, :],
            lb_all,
            hgrn_norm_w[l].astype(F32)[None, :],
            w_out[l].astype(BF16),
            final_norm_w.astype(F32)[None, :],
        ))

    bp, tp, _ = x_prompt.shape
    lc_p = CHUNK if tp % CHUNK == 0 else tp
    tm_p = 256 if tp % 256 == 0 else lc_p
    empty = (jnp.zeros((bp, HEADS, QK, HD), F32), jnp.zeros((bp, HEADS, QK), F32),
             jnp.full((bp, 1, LANES), NEG_INIT, F32), jnp.zeros((bp, HEADS, QK, HD), F32),
             jnp.zeros((bp, S_GROUPS, S_STATE, S_GROUP_LANES), F32),
             jnp.zeros((bp, CONV_W - 1, S_CONV_DIM), F32), jnp.zeros((bp, HEADS, HD, HD), F32))
    cos_p, sin_p = _rope_tables(jnp.arange(tp))
    y_prompt, p_st = _run_group(x_prompt, cos_p, sin_p, [empty] * depth, layer_weights,
                                carry=True, tm=tm_p, lc=lc_p)

    bs_, ts, _ = x_sample.shape
    lc_s = CHUNK if ts % CHUNK == 0 else ts
    carried = []
    for l in range(depth):
        mm = jnp.pad(state_mlstm_m[l].astype(F32), ((0, 0), (0, LANES - HEADS)))[:, None, :]
        carried.append((state_mlstm_c[l].astype(F32), state_mlstm_n[l].astype(F32), mm,
                        state_ret[l].astype(F32), _ssd_state_to_lanes(state_ssd[l].astype(F32)),
                        cache_ssd_conv[l].astype(F32), state_hgrn[l].astype(F32)))
    cos_s, sin_s = _rope_tables(jnp.tile(PAST_LEN + jnp.arange(ts), bs_))
    y_sample, s_st = _run_group(x_sample.reshape(1, bs_ * ts, D_MODEL), cos_s, sin_s, carried,
                                layer_weights, carry=False, tm=bs_ * ts, lc=lc_s)
    y_sample = y_sample.reshape(bs_, ts, D_MODEL)

    return (y_prompt, y_sample, *p_st, *s_st)
```

```python
import functools

import numpy as np
import jax
import jax.numpy as jnp
from jax import lax
from jax.experimental import pallas as pl
from jax.experimental.pallas import tpu as pltpu

F32 = jnp.float32
BF16 = jnp.bfloat16

D_MODEL = 1024
GROUP_WIDTH = 512
MIX_WIDTH = 4 * GROUP_WIDTH
EPS = 1e-6
NEG_INIT = -1e30
NEG_BIG = -1e30
CHUNK = 64
PAST_LEN = 1024
ROPE_BASE = 10000.0
HEADS = 4
QK = 64
HD = 128
S_HEADS = 8
S_HD = 64
S_GROUPS = 2
S_STATE = 128
S_GROUP_LANES = (S_HEADS // S_GROUPS) * S_HD
CONV_W = 4
S_CONV_DIM = 1024
H_BLOCK = 16
MXU_COLS = 256
HGRN_SAFE_EXP = 60.0
LANES = 128
SUBLANES = 8
VMEM_LIMIT_BYTES = 56 * 1024 * 1024

IN_SIZES = (256, 256, 512, 512, 512, 8, 256, 256, 512, 512, 512, 1024, 8, 512, 512, 512, 512)
SEC_NAMES = ("mq", "mk", "mv", "mo", "mz", "mif", "rq", "rk", "rv", "rg", "sz", "sxbc", "sdt",
             "hq", "hf", "hi", "hg")


def _padded(width):
    return -(-width // LANES) * LANES


SEC = {}
_off = 0
for _n, _w in zip(SEC_NAMES, IN_SIZES):
    SEC[_n] = (_off, _padded(_w))
    _off += _padded(_w)
PROJ_COLS = _off
PROJ_PAD = -(-PROJ_COLS // MXU_COLS) * MXU_COLS


def _dot(a, b):
    return jnp.dot(a, b, preferred_element_type=F32)


def _dot_nt(a, b):
    return lax.dot_general(a, b, (((1,), (1,)), ((), ())), preferred_element_type=F32)


def _dot_tn(a, b):
    return lax.dot_general(a, b, (((0,), (0,)), ((), ())), preferred_element_type=F32)


def _sigmoid(x):
    return 0.5 * jnp.tanh(0.5 * x) + 0.5


def _silu(x):
    return x * _sigmoid(x)


def _softplus(x):
    return jnp.maximum(x, 0.0) + jnp.log1p(jnp.exp(-jnp.abs(x)))


def _log_sigmoid(x):
    return jnp.minimum(x, 0.0) - jnp.log1p(jnp.exp(-jnp.abs(x)))


def _cumsum_rows(x, n):
    rid = lax.broadcasted_iota(jnp.int32, x.shape, 0)
    s = 1
    while s < n:
        x = x + jnp.where(rid >= s, pltpu.roll(x, s, 0), 0.0)
        s *= 2
    return x


def _cumsum_mxu(x, tri):
    hi = x.astype(BF16)
    r1 = x - hi.astype(F32)
    mid = r1.astype(BF16)
    lo = (r1 - mid.astype(F32)).astype(BF16)
    return _dot(tri, hi) + _dot(tri, mid) + _dot(tri, lo)


def _cummax_rows(x, n):
    rid = lax.broadcasted_iota(jnp.int32, x.shape, 0)
    s = 1
    while s < n:
        x = jnp.maximum(x, jnp.where(rid >= s, pltpu.roll(x, s, 0), x))
        s *= 2
    return x


def _col_to_row(col, eye):
    return jnp.sum(jnp.where(eye, col, 0.0), axis=0, keepdims=True)


def _expand_heads64(c):
    lane = lax.broadcasted_iota(jnp.int32, (c.shape[0], LANES), 1)
    parts = [jnp.where(lane < S_HD, c[:, 2 * j:2 * j + 1], c[:, 2 * j + 1:2 * j + 2])
             for j in range(S_HEADS // 2)]
    return jnp.concatenate(parts, axis=1)


def _rms(h, width):
    return h * lax.rsqrt(jnp.sum(h * h, axis=-1, keepdims=True) * (1.0 / width) + EPS)


def _norm_rows(x, w):
    return (x * lax.rsqrt(jnp.sum(x * x, axis=-1, keepdims=True) * (1.0 / D_MODEL) + EPS)
            * w).astype(BF16)


def _layer_kernel(x_ref, xn_ref, cos_ref, sin_ref, mc_i, mn_i, mm_i, rs_i, sh_i, sb_i, hs_i,
                  nw_ref, win_ref, gb_ref, mnw_ref, cw_ref, cb_ref, dtb_ref, alog_ref, dsk_ref,
                  snw_ref, lb_ref, hnw_ref, wout_ref, fnw_ref,
                  y_ref, mc_o, mn_o, mm_o, rs_o, sh_o, sb_o, hs_o,
                  proj_ref, stage_ref, hnn_ref, mix_ref, ext_ref, hnew_ref,
                  *, layer, last, carry, tm, lc, sb, nt, pipelined):
    L = lc
    nch = tm // lc
    t_id = pl.program_id(1)
    if pipelined:
        step = pl.program_id(0) * nt + t_id
    n_pieces = -(-PROJ_PAD // MXU_COLS)
    piece_hi = [min(((c + 1) * n_pieces + nch - 1) // nch * MXU_COLS, PROJ_PAD) for c in range(nch)]
    col_lo, col_hi = [0] + piece_hi[:-1], piece_hi
    hgrn_col0 = SEC["hq"][0]

    @pl.when(t_id == 0)
    def _init():
        mc_o[...] = mc_i[...]
        mn_o[...] = mn_i[...]
        mm_o[...] = mm_i[...]
        rs_o[...] = rs_i[...]
        sh_o[...] = sh_i[...]
        sb_o[...] = sb_i[...]
        for s in range(sb):
            for h in range(HEADS):
                hs_o[s, h] = hs_i[s, h].T

    x = x_ref[0]
    if pipelined:
        @pl.when(step == 0)
        def _first_proj():
            proj_ref[...] = _dot(_norm_rows(x, nw_ref[...]), win_ref[...])
        hnn_ref[...] = _norm_rows(xn_ref[0], nw_ref[...])
    else:
        proj_ref[...] = _dot(_norm_rows(x, nw_ref[...]), win_ref[...])

    ri = lax.broadcasted_iota(jnp.int32, (L, L), 0)
    ci = lax.broadcasted_iota(jnp.int32, (L, L), 1)
    tril = ri >= ci
    eye = ri == ci
    tri_bf = jnp.where(tril, 1.0, 0.0).astype(BF16)
    dti = (ri - ci).astype(F32)
    rowf = lax.broadcasted_iota(jnp.int32, (L, 1), 0).astype(F32)
    ret_dec, ret_rowdec, ret_kdec, ret_cdec = [], [], [], []
    for h in range(HEADS):
        lg = float(np.log1p(-(np.float32(2.0) ** np.float32(-5.0 - h))).astype(np.float32))
        ret_dec.append(jnp.exp(jnp.where(tril, dti * lg, NEG_BIG)))
        ret_rowdec.append(jnp.exp((rowf + 1.0) * lg))
        ret_kdec.append(jnp.exp((L - 1.0 - rowf) * lg))
        ret_cdec.append(float(np.exp(np.float32(L) * np.float32(lg))))
    lane256 = lax.broadcasted_iota(jnp.int32, (L, S_GROUP_LANES), 1)
    lane_in_seg = lane256 % QK
    head_of_lane = lane256 // S_HD
    row16 = lax.broadcasted_iota(jnp.int32, (H_BLOCK, 1), 0)

    lbp = lb_ref[...]
    lbe = jnp.exp(lbp - jnp.max(lbp, axis=0, keepdims=True))
    lbs = lbe / jnp.sum(lbe, axis=0, keepdims=True)
    lbc = lbs[0:1]
    for i in range(1, layer + 1):
        lbc = lbc + lbs[i:i + 1]
    lbr = lbc - lbs[0:1]

    def rot(v, cosf, sinf):
        sw = jnp.where(lane_in_seg < QK // 2,
                       pltpu.roll(v, S_GROUP_LANES - QK // 2, 1), pltpu.roll(v, QK // 2, 1))
        return v * cosf + sw * sinf

    def hgrn_inputs(rsel):
        def col(name):
            off, w = SEC[name]
            return proj_ref[rsel, off:off + w]
        hf = col("hf")
        zf = jnp.exp(-jnp.abs(hf))
        rf = 1.0 / (1.0 + zf)
        zr = zf * rf
        f_pos = hf >= 0.0
        kk = (1.0 - lbr) * jnp.where(f_pos, zr, rf)
        lfh = jnp.log(lbr + (1.0 - lbr) * jnp.where(f_pos, rf, zr))
        return col("hq") * (HD ** -0.5), kk, lfh, col("hi"), _silu(col("hg"))

    def out_proj_piece(r_lo, r_hi, c_lo, k_lo=0, k_hi=MIX_WIDTH):
        def emit():
            cols = slice(c_lo, c_lo + MXU_COLS)
            part = _dot(mix_ref[r_lo:r_hi, k_lo:k_hi], wout_ref[k_lo:k_hi, cols])
            if k_lo == 0:
                y_ref[0, r_lo:r_hi, cols] = part + x_ref[0, r_lo:r_hi, cols]
            else:
                y_ref[0, r_lo:r_hi, cols] += part
        return emit

    def finish_rows(r_lo, r_hi):
        if last:
            y_ref[0, r_lo:r_hi, :] = _rms(y_ref[0, r_lo:r_hi, :], D_MODEL) * fnw_ref[...]

    def chunk(c, hgrn_state, extra_mxu_work):
        r0 = c * L
        rows = slice(r0, r0 + L)
        s = 0 if carry else c

        def sec(name):
            off, w = SEC[name]
            return proj_ref[rows, off:off + w]

        def proj_piece(o, w):
            def emit():
                stage_ref[:, o:o + w] = _dot(hnn_ref[...], win_ref[:, o:o + w])
            return emit

        pieces = ([(0, proj_piece(o, min(MXU_COLS, col_hi[c] - o))) for o in range(col_lo[c], col_hi[c], MXU_COLS)]
                  if pipelined else [])
        pieces = pieces + list(extra_mxu_work)
        n_slots = 2 * HEADS + S_GROUPS
        per_slot = -(-len(pieces) // n_slots)

        def next_proj_piece(stage=0, limit=None):
            issued = 0
            for entry in list(pieces):
                if issued == (per_slot if limit is None else limit):
                    break
                if entry[0] <= stage:
                    pieces.remove(entry)
                    entry[1]()
                    issued += 1

        g = sec("mif") + gb_ref[...]
        ig = g
        lf = _log_sigmoid(pltpu.roll(g, LANES - HEADS, 1))
        b = _cumsum_rows(lf, L)
        a = ig - b
        m_prev = mm_o[s]
        big_a = jnp.maximum(m_prev, _cummax_rows(a, L))
        m_t = b + big_a
        wi = jnp.exp(m_prev - big_a)
        em = jnp.exp(-m_t)
        a_last = big_a[L - 1:L]
        wk = jnp.exp(a - a_last)
        wc = jnp.exp(m_prev - a_last)
        mm_o[s] = m_t[L - 1:L]
        q_all = sec("mq") * (QK ** -0.5)
        k_all = sec("mk")
        v_all = sec("mv")
        o_all = sec("mo")
        z_all = sec("mz")

        def mlstm_head(h):
            ks = slice(h * QK, (h + 1) * QK)
            vs = slice(h * HD, (h + 1) * HD)
            qh, kh = q_all[:, ks], k_all[:, ks]
            qb, kb, vb = qh.astype(BF16), kh.astype(BF16), v_all[:, vs].astype(BF16)
            a_row = _col_to_row(a[:, h:h + 1], eye)
            e = jnp.exp(jnp.where(tril, a_row - big_a[:, h:h + 1], NEG_BIG))
            sc = _dot_nt(qb, kb) * e
            c_prev = mc_o[s, h]
            n_prev = mn_o[s, pl.ds(h, 1), :]
            wi_c = wi[:, h:h + 1]
            num = _dot(sc.astype(BF16), vb) + wi_c * _dot(qb, c_prev.astype(BF16))
            den = (jnp.sum(sc, axis=-1, keepdims=True)
                   + wi_c * jnp.sum(qh * n_prev, axis=-1, keepdims=True))
            hh = num / jnp.maximum(jnp.abs(den), em[:, h:h + 1])
            kw = kh * wk[:, h:h + 1]
            wc_h = wc[:, h:h + 1]
            mc_o[s, h] = wc_h * c_prev + _dot_tn(kw.astype(BF16), vb)
            mn_o[s, pl.ds(h, 1), :] = wc_h * n_prev + jnp.sum(kw, axis=0, keepdims=True)
            out = (_rms(hh, HD) * mnw_ref[:, vs]) * _sigmoid(o_all[:, vs]) * _silu(z_all[:, vs])
            mix_ref[rows, h * HD:(h + 1) * HD] = out.astype(BF16)

        cosf = cos_ref[rows, :]
        sinf = sin_ref[rows, :]
        rq = rot(sec("rq"), cosf, sinf) * (QK ** -0.5)
        rk = rot(sec("rk"), cosf, sinf)
        rv = sec("rv")
        rg = sec("rg")

        def ret_head(h):
            ks = slice(h * QK, (h + 1) * QK)
            vs = slice(h * HD, (h + 1) * HD)
            kh = rk[:, ks]
            qb, kb, vb = rq[:, ks].astype(BF16), kh.astype(BF16), rv[:, vs].astype(BF16)
            sc = _dot_nt(qb, kb) * ret_dec[h]
            s_prev = rs_o[s, h]
            o = _dot(sc.astype(BF16), vb) + ret_rowdec[h] * _dot(qb, s_prev.astype(BF16))
            rs_o[s, h] = ret_cdec[h] * s_prev + _dot_tn((kh * ret_kdec[h]).astype(BF16), vb)
            cen = o - jnp.sum(o, axis=-1, keepdims=True) * (1.0 / HD)
            out = _rms(cen, HD) * _silu(rg[:, vs])
            mix_ref[rows, GROUP_WIDTH + h * HD:GROUP_WIDTH + (h + 1) * HD] = out.astype(BF16)

        for h in range(HEADS):
            mlstm_head(h)
            ret_head(h)
            next_proj_piece()

        ext_ref[SUBLANES - (CONV_W - 1):SUBLANES, :] = sb_o[s]
        ext_ref[SUBLANES:SUBLANES + L, :] = sec("sxbc")
        acc = cb_ref[...] + cw_ref[CONV_W - 1:CONV_W, :] * ext_ref[SUBLANES:SUBLANES + L, :]
        for j in range(CONV_W - 1):
            st = SUBLANES - (CONV_W - 1) + j
            acc = acc + cw_ref[j:j + 1, :] * ext_ref[st:st + L, :]
        sb_o[s] = ext_ref[SUBLANES + L - (CONV_W - 1):SUBLANES + L, :]
        xbc = _silu(acc)
        xs = xbc[:, :GROUP_WIDTH]
        bs = xbc[:, GROUP_WIDTH:GROUP_WIDTH + S_GROUPS * S_STATE]
        cs = xbc[:, GROUP_WIDTH + S_GROUPS * S_STATE:]
        dtc = _softplus(sec("sdt") + dtb_ref[...])
        la = dtc * (-jnp.exp(alog_ref[...]))
        bcum = _cumsum_rows(la, L)
        b_last = bcum[L - 1:L]
        xv = xs * _expand_heads64(dtc)
        xw = xv * _expand_heads64(jnp.exp(b_last - bcum))
        eb_e = _expand_heads64(jnp.exp(bcum))
        ebl_e = _expand_heads64(jnp.exp(b_last))
        ys = []
        for gi in range(S_GROUPS):
            gl = slice(gi * S_GROUP_LANES, (gi + 1) * S_GROUP_LANES)
            cq = cs[:, gi * S_STATE:(gi + 1) * S_STATE].astype(BF16)
            bk = bs[:, gi * S_STATE:(gi + 1) * S_STATE].astype(BF16)
            h_prev = sh_o[s, gi]
            y_g = eb_e[:, gl] * _dot(cq, h_prev.astype(BF16))
            xv_g = xv[:, gl]
            cbm = _dot_nt(cq, bk)
            for hh in range(S_HEADS // S_GROUPS):
                h = gi * (S_HEADS // S_GROUPS) + hh
                b_col = bcum[:, h:h + 1]
                dec = jnp.exp(jnp.where(tril, b_col - _col_to_row(b_col, eye), NEG_BIG))
                xm = jnp.where(head_of_lane == hh, xv_g, 0.0).astype(BF16)
                y_g = y_g + _dot((cbm * dec).astype(BF16), xm)
            sh_o[s, gi] = ebl_e[:, gl] * h_prev + _dot_tn(bk, xw[:, gl].astype(BF16))
            ys.append(y_g)
            next_proj_piece(stage=1)
        y_s = jnp.concatenate(ys, axis=1) + dsk_ref[...] * xs
        y_s = y_s * _silu(sec("sz"))
        for gi in range(S_GROUPS):
            gl = slice(gi * S_GROUP_LANES, (gi + 1) * S_GROUP_LANES)
            out = _rms(y_s[:, gl], S_GROUP_LANES) * snw_ref[:, gl]
            mix_ref[rows, 2 * GROUP_WIDTH + gi * S_GROUP_LANES:
                    2 * GROUP_WIDTH + (gi + 1) * S_GROUP_LANES] = out.astype(BF16)

        hq, kk, lfh, hv, hg_gate = hgrn_inputs(rows)
        bch = _cumsum_mxu(lfh, tri_bf)
        anchor = bch[L // 2 - 1:L // 2]
        blast = bch[L - 1:L]
        dq = bch - anchor
        worst = jnp.max(jnp.max(jnp.abs(dq), axis=1, keepdims=True), axis=0, keepdims=True)
        unsafe = jnp.logical_not(worst[0, 0] <= HGRN_SAFE_EXP)
        qe = hq * jnp.exp(dq)
        ke = kk * jnp.exp(-dq)
        qs = qe * jnp.exp(anchor)
        kend = ke * jnp.exp(blast - anchor)
        ebl = jnp.exp(blast)
        new_state = []
        for h in range(HEADS):
            vs = slice(h * HD, (h + 1) * HD)
            vb = hv[:, vs].astype(BF16)
            st_prev = hgrn_state[h]
            sc = jnp.where(tril, _dot_nt(qe[:, vs].astype(BF16), ke[:, vs].astype(BF16)), 0.0)
            o = _dot(sc.astype(BF16), vb) + _dot_nt(qs[:, vs].astype(BF16), st_prev.astype(BF16))
            new_state.append(st_prev * ebl[:, vs] + _dot_tn(vb, kend[:, vs].astype(BF16)))
            out = _rms(o, HD) * hnw_ref[:, vs] * hg_gate[:, vs]
            mix_ref[rows, 3 * GROUP_WIDTH + h * HD:3 * GROUP_WIDTH + (h + 1) * HD] = out.astype(BF16)
            next_proj_piece(stage=2)
        next_proj_piece(stage=3, limit=len(pieces))
        return new_state, unsafe

    half = tm // 2
    early_out = half % L == 0 and nch >= 2 and D_MODEL % MXU_COLS == 0
    out_cols = range(0, D_MODEL, MXU_COLS)
    first_half = [out_proj_piece(0, half, c_lo) for c_lo in out_cols] if early_out else []
    late_chunks = list(range(half // L, nch)) if early_out else []
    any_unsafe = None
    state = None
    for c in range(nch):
        if state is None or not carry:
            state = [hs_o[0 if carry else c, h] for h in range(HEADS)]
        extra = []
        if c in late_chunks:
            k = late_chunks.index(c)
            per = -(-len(first_half) // len(late_chunks))
            extra = [(0, emit) for emit in first_half[k * per:(k + 1) * per]]
        state, unsafe = chunk(c, state, extra)
        any_unsafe = unsafe if any_unsafe is None else jnp.logical_or(any_unsafe, unsafe)
        if not carry or c == nch - 1:
            for h in range(HEADS):
                hnew_ref[0 if carry else c, h] = state[h]
        if pipelined:
            ready = [(c, q) for q in range(c + 1)] + [(r, c) for r in range(c)]
            for r, q in ready:
                lo, hi = col_lo[q], min(col_hi[q], hgrn_col0)
                if lo < hi:
                    proj_ref[r * L:(r + 1) * L, lo:hi] = stage_ref[r * L:(r + 1) * L, lo:hi]

    @pl.when(any_unsafe)
    def _exact_hgrn():
        hnew_ref[...] = hs_o[...]

        def block(i, carry_val):
            r16 = pl.multiple_of(i * H_BLOCK, H_BLOCK)
            brow = pl.ds(r16, H_BLOCK)
            si = 0 if carry else (i if L == H_BLOCK else i // (L // H_BLOCK))
            hq, kk, lfh, hv, hg_gate = hgrn_inputs(brow)
            b16 = _cumsum_rows(lfh, H_BLOCK)
            bl = b16[H_BLOCK - 1:H_BLOCK]
            qe = hq * jnp.exp(b16)
            kend = kk * jnp.exp(bl - b16)
            ebl = jnp.exp(bl)
            for h in range(HEADS):
                vs = slice(h * HD, (h + 1) * HD)
                st = hnew_ref[si, h]
                bh, qh, kh, vh = b16[:, vs], hq[:, vs], kk[:, vs], hv[:, vs]
                o = _dot_nt(qe[:, vs].astype(BF16), st.astype(BF16))
                for sp in range(H_BLOCK):
                    w = jnp.exp(bh - bh[sp:sp + 1]) * qh * kh[sp:sp + 1]
                    scol = jnp.where(row16 >= sp, jnp.sum(w, axis=-1, keepdims=True), 0.0)
                    o = o + scol * vh[sp:sp + 1]
                hnew_ref[si, h] = st * ebl[:, vs] + _dot_tn(vh.astype(BF16), kend[:, vs].astype(BF16))
                out = _rms(o, HD) * hnw_ref[:, vs] * hg_gate[:, vs]
                mix_ref[brow, 3 * GROUP_WIDTH + h * HD:3 * GROUP_WIDTH + (h + 1) * HD] = out.astype(BF16)
            return carry_val

        lax.fori_loop(0, tm // H_BLOCK, block, 0)
        for emit in first_half:
            emit()

    hs_o[...] = hnew_ref[...]

    if pipelined:
        proj_ref[:, hgrn_col0:] = stage_ref[:, hgrn_col0:]
    if early_out:
        finish_rows(0, half)
        for c_lo in out_cols:
            out_proj_piece(half, tm, c_lo)()
        finish_rows(half, tm)
    else:
        y = _dot(mix_ref[...], wout_ref[...]) + x
        if last:
            y = _rms(y, D_MODEL) * fnw_ref[...]
        y_ref[0] = y

    @pl.when(t_id == nt - 1)
    def _fin():
        for s in range(sb):
            for h in range(HEADS):
                hs_o[s, h] = hs_o[s, h].T


def _const_spec(shape):
    nd = len(shape)
    return pl.BlockSpec(shape, lambda b, t, _nd=nd: (0,) * _nd, pipeline_mode=pl.Buffered(1))


def _layer_call(x, cosf, sinf, states, weights, *, layer, last, carry, tm, lc):
    bsz, t_len, _ = x.shape
    nt = t_len // tm
    sb = 1 if carry else states[0].shape[0]
    grid = (bsz, nt)
    n_steps = bsz * nt
    pipelined = n_steps > 1

    def state_spec(arr):
        blk = (sb,) + arr.shape[1:]
        nd = arr.ndim
        return pl.BlockSpec(blk, lambda b, t, _nd=nd: (b,) + (0,) * (_nd - 1))

    def next_tile(b, t):
        nxt = jnp.minimum(b * nt + t + 1, n_steps - 1)
        return (nxt // nt, nxt % nt, 0)

    in_specs = ([pl.BlockSpec((1, tm, D_MODEL), lambda b, t: (b, t, 0)),
                 pl.BlockSpec((1, tm, D_MODEL), next_tile),
                 pl.BlockSpec((tm, S_GROUP_LANES), lambda b, t: (t, 0)),
                 pl.BlockSpec((tm, S_GROUP_LANES), lambda b, t: (t, 0))]
                + [state_spec(a) for a in states]
                + [_const_spec(w.shape) for w in weights])
    out_shape = ([jax.ShapeDtypeStruct(x.shape, F32)]
                 + [jax.ShapeDtypeStruct(a.shape, F32) for a in states])
    out_specs = ([pl.BlockSpec((1, tm, D_MODEL), lambda b, t: (b, t, 0))]
                 + [state_spec(a) for a in states])
    kern = functools.partial(_layer_kernel, layer=layer, last=last, carry=carry, tm=tm, lc=lc,
                             sb=sb, nt=nt, pipelined=pipelined)
    outs = pl.pallas_call(
        kern,
        grid=grid,
        in_specs=in_specs,
        out_specs=out_specs,
        out_shape=out_shape,
        scratch_shapes=[pltpu.VMEM((tm, PROJ_PAD), F32),
                        pltpu.VMEM((tm if pipelined else SUBLANES, PROJ_PAD), F32),
                        pltpu.VMEM((tm, D_MODEL), BF16),
                        pltpu.VMEM((tm, MIX_WIDTH), BF16),
                        pltpu.VMEM((lc + SUBLANES, S_CONV_DIM), F32),
                        pltpu.VMEM((sb, HEADS, HD, HD), F32)],
        compiler_params=pltpu.CompilerParams(
            dimension_semantics=("arbitrary", "arbitrary"),
            vmem_limit_bytes=VMEM_LIMIT_BYTES),
        name=f"hybrid_layer{layer}_{'prompt' if carry else 'sample'}",
    )(x, x, cosf, sinf, *states, *weights)
    return outs[0], outs[1:]


def _pack_w_in(w):
    parts, off = [], 0
    for width in IN_SIZES:
        p = w[:, off:off + width]
        if _padded(width) != width:
            p = jnp.pad(p, ((0, 0), (0, _padded(width) - width)))
        parts.append(p)
        off += width
    parts.append(jnp.zeros((w.shape[0], PROJ_PAD - PROJ_COLS), w.dtype))
    return jnp.concatenate(parts, axis=1).astype(BF16)


def _pad_lanes(v):
    return jnp.pad(v.astype(F32), (0, LANES - v.shape[0]))[None, :]


def _rope_tables(pos):
    half = QK // 2
    freqs = np.float64(ROPE_BASE) ** (-np.arange(half, dtype=np.float64) / half)
    ang = np.asarray(pos, np.float64)[:, None] * freqs[None, :]
    cos, sin = np.cos(ang), np.sin(ang)
    cosf = np.tile(np.concatenate([cos, cos], axis=1), (1, HEADS))
    sinf = np.tile(np.concatenate([-sin, sin], axis=1), (1, HEADS))
    return jnp.asarray(cosf, F32), jnp.asarray(sinf, F32)


def _ssd_state_to_lanes(h):
    b = h.shape[0]
    hpg = S_HEADS // S_GROUPS
    return h.reshape(b, S_GROUPS, hpg, S_STATE, S_HD).transpose(0, 1, 3, 2, 4).reshape(
        b, S_GROUPS, S_STATE, S_GROUP_LANES)


def _ssd_state_from_lanes(h):
    b = h.shape[0]
    hpg = S_HEADS // S_GROUPS
    return h.reshape(b, S_GROUPS, S_STATE, hpg, S_HD).transpose(0, 1, 3, 2, 4).reshape(
        b, S_HEADS, S_STATE, S_HD)


def _run_group(x, cosf, sinf, init_states, layer_weights, *, carry, tm, lc):
    depth = len(layer_weights)
    per_layer = []
    for l in range(depth):
        x, st = _layer_call(x, cosf, sinf, init_states[l], layer_weights[l], layer=l,
                            last=(l == depth - 1), carry=carry, tm=tm, lc=lc)
        mc, mn, mm, rs, sh, sbuf, hs = st
        per_layer.append((mc, mn, mm[:, 0, :HEADS], rs, _ssd_state_from_lanes(sh), sbuf, hs))
    stacked = [jnp.stack([st[i] for st in per_layer], axis=0) for i in range(7)]
    return x, stacked


def kernel(x_prompt, x_sample, state_mlstm_c, state_mlstm_n, state_mlstm_m, state_ret, state_ssd,
           cache_ssd_conv, state_hgrn, norm_w, w_in, mlstm_gate_b, mlstm_norm_w, ssd_conv_w,
           ssd_conv_b, ssd_dt_bias, ssd_a_log, ssd_d, ssd_norm_w, hgrn_lower_bounds, hgrn_norm_w,
           w_out, final_norm_w):
    depth = w_in.shape[0]
    lb_all = hgrn_lower_bounds.astype(F32)
    layer_weights = []
    for l in range(depth):
        layer_weights.append((
            norm_w[l].astype(F32)[None, :],
            _pack_w_in(w_in[l]),
            _pad_lanes(mlstm_gate_b[l]),
            mlstm_norm_w[l].astype(F32)[None, :],
            ssd_conv_w[l].astype(F32),
            ssd_conv_b[l].astype(F32)[None, :],
            _pad_lanes(ssd_dt_bias[l]),
            _pad_lanes(ssd_a_log[l]),
            jnp.repeat(ssd_d[l].astype(F32), S_HD)[None, :],
            ssd_norm_w[l].astype(F32)[None, :],
            lb_all,
            hgrn_norm_w[l].astype(F32)[None, :],
            w_out[l].astype(BF16),
            final_norm_w.astype(F32)[None, :],
        ))

    bp, tp, _ = x_prompt.shape
    lc_p = CHUNK if tp % CHUNK == 0 else tp
    tm_p = 256 if tp % 256 == 0 else lc_p
    empty = (jnp.zeros((bp, HEADS, QK, HD), F32), jnp.zeros((bp, HEADS, QK), F32),
             jnp.full((bp, 1, LANES), NEG_INIT, F32), jnp.zeros((bp, HEADS, QK, HD), F32),
             jnp.zeros((bp, S_GROUPS, S_STATE, S_GROUP_LANES), F32),
             jnp.zeros((bp, CONV_W - 1, S_CONV_DIM), F32), jnp.zeros((bp, HEADS, HD, HD), F32))
    cos_p, sin_p = _rope_tables(np.arange(tp))
    y_prompt, p_st = _run_group(x_prompt, cos_p, sin_p, [empty] * depth, layer_weights,
                                carry=True, tm=tm_p, lc=lc_p)

    bs_, ts, _ = x_sample.shape
    lc_s = CHUNK if ts % CHUNK == 0 else ts
    carried = []
    for l in range(depth):
        mm = jnp.pad(state_mlstm_m[l].astype(F32), ((0, 0), (0, LANES - HEADS)))[:, None, :]
        carried.append((state_mlstm_c[l].astype(F32), state_mlstm_n[l].astype(F32), mm,
                        state_ret[l].astype(F32), _ssd_state_to_lanes(state_ssd[l].astype(F32)),
                        cache_ssd_conv[l].astype(F32), state_hgrn[l].astype(F32)))
    cos_s, sin_s = _rope_tables(np.tile(PAST_LEN + np.arange(ts), bs_))
    y_sample, s_st = _run_group(x_sample.reshape(1, bs_ * ts, D_MODEL), cos_s, sin_s, carried,
                                layer_weights, carry=False, tm=bs_ * ts, lc=lc_s)
    y_sample = y_sample.reshape(bs_, ts, D_MODEL)

    return (y_prompt, y_sample, *p_st, *s_st)
```

```python
import functools

import numpy as np
import jax
import jax.numpy as jnp
from jax import lax
from jax.experimental import pallas as pl
from jax.experimental.pallas import tpu as pltpu

F32 = jnp.float32
BF16 = jnp.bfloat16

D_MODEL = 1024
GROUP_WIDTH = 512
MIX_WIDTH = 4 * GROUP_WIDTH
EPS = 1e-6
NEG_INIT = -1e30
NEG_BIG = -1e30
CHUNK = 64
PAST_LEN = 1024
ROPE_BASE = 10000.0
HEADS = 4
QK = 64
HD = 128
S_HEADS = 8
S_HD = 64
S_GROUPS = 2
S_STATE = 128
S_GROUP_LANES = (S_HEADS // S_GROUPS) * S_HD
CONV_W = 4
S_CONV_DIM = 1024
H_BLOCK = 16
MXU_COLS = 256
HGRN_SAFE_EXP = 60.0
LANES = 128
SUBLANES = 8
VMEM_LIMIT_BYTES = 56 * 1024 * 1024

IN_SIZES = (256, 256, 512, 512, 512, 8, 256, 256, 512, 512, 512, 1024, 8, 512, 512, 512, 512)
SEC_NAMES = ("mq", "mk", "mv", "mo", "mz", "mif", "rq", "rk", "rv", "rg", "sz", "sxbc", "sdt",
             "hq", "hf", "hi", "hg")


def _padded(width):
    return -(-width // LANES) * LANES


SEC = {}
_off = 0
for _n, _w in zip(SEC_NAMES, IN_SIZES):
    SEC[_n] = (_off, _padded(_w))
    _off += _padded(_w)
PROJ_COLS = _off
PROJ_PAD = -(-PROJ_COLS // MXU_COLS) * MXU_COLS


def _dot(a, b):
    return jnp.dot(a, b, preferred_element_type=F32)


def _dot_nt(a, b):
    return lax.dot_general(a, b, (((1,), (1,)), ((), ())), preferred_element_type=F32)


def _dot_tn(a, b):
    return lax.dot_general(a, b, (((0,), (0,)), ((), ())), preferred_element_type=F32)


def _sigmoid(x):
    return 0.5 * jnp.tanh(0.5 * x) + 0.5


def _silu(x):
    return x * _sigmoid(x)


def _softplus(x):
    return jnp.maximum(x, 0.0) + jnp.log1p(jnp.exp(-jnp.abs(x)))


def _log_sigmoid(x):
    return jnp.minimum(x, 0.0) - jnp.log1p(jnp.exp(-jnp.abs(x)))


def _cumsum_rows(x, n):
    rid = lax.broadcasted_iota(jnp.int32, x.shape, 0)
    s = 1
    while s < n:
        x = x + jnp.where(rid >= s, pltpu.roll(x, s, 0), 0.0)
        s *= 2
    return x


def _cumsum_mxu(x, tri):
    hi = x.astype(BF16)
    r1 = x - hi.astype(F32)
    mid = r1.astype(BF16)
    lo = (r1 - mid.astype(F32)).astype(BF16)
    return _dot(tri, hi) + _dot(tri, mid) + _dot(tri, lo)


def _cummax_rows(x, n):
    rid = lax.broadcasted_iota(jnp.int32, x.shape, 0)
    s = 1
    while s < n:
        x = jnp.maximum(x, jnp.where(rid >= s, pltpu.roll(x, s, 0), x))
        s *= 2
    return x


def _col_to_row(col, eye):
    return jnp.sum(jnp.where(eye, col, 0.0), axis=0, keepdims=True)


def _expand_heads64(c):
    lane = lax.broadcasted_iota(jnp.int32, (c.shape[0], LANES), 1)
    parts = [jnp.where(lane < S_HD, c[:, 2 * j:2 * j + 1], c[:, 2 * j + 1:2 * j + 2])
             for j in range(S_HEADS // 2)]
    return jnp.concatenate(parts, axis=1)


def _rms(h, width):
    return h * lax.rsqrt(jnp.sum(h * h, axis=-1, keepdims=True) * (1.0 / width) + EPS)


def _norm_rows(x, w):
    return (x * lax.rsqrt(jnp.sum(x * x, axis=-1, keepdims=True) * (1.0 / D_MODEL) + EPS)
            * w).astype(BF16)


def _layer_kernel(x_ref, xn_ref, cos_ref, sin_ref, mc_i, mn_i, mm_i, rs_i, sh_i, sb_i, hs_i,
                  nw_ref, win_ref, gb_ref, mnw_ref, cw_ref, cb_ref, dtb_ref, alog_ref, dsk_ref,
                  snw_ref, lb_ref, hnw_ref, wout_ref, fnw_ref,
                  y_ref, mc_o, mn_o, mm_o, rs_o, sh_o, sb_o, hs_o,
                  proj_ref, stage_ref, hnn_ref, mix_ref, ext_ref, hnew_ref,
                  *, layer, last, carry, tm, lc, sb, nt, pipelined):
    L = lc
    nch = tm // lc
    t_id = pl.program_id(1)
    if pipelined:
        step = pl.program_id(0) * nt + t_id
    n_pieces = -(-PROJ_PAD // MXU_COLS)
    piece_hi = [min(((c + 1) * n_pieces + nch - 1) // nch * MXU_COLS, PROJ_PAD) for c in range(nch)]
    col_lo, col_hi = [0] + piece_hi[:-1], piece_hi
    hgrn_col0 = SEC["hq"][0]

    @pl.when(t_id == 0)
    def _init():
        mc_o[...] = mc_i[...]
        mn_o[...] = mn_i[...]
        mm_o[...] = mm_i[...]
        rs_o[...] = rs_i[...]
        sh_o[...] = sh_i[...]
        sb_o[...] = sb_i[...]
        for s in range(sb):
            for h in range(HEADS):
                hs_o[s, h] = hs_i[s, h].T

    x = x_ref[0]
    if pipelined:
        @pl.when(step == 0)
        def _first_proj():
            proj_ref[...] = _dot(_norm_rows(x, nw_ref[...]), win_ref[...])
        hnn_ref[...] = _norm_rows(xn_ref[0], nw_ref[...])
    else:
        proj_ref[...] = _dot(_norm_rows(x, nw_ref[...]), win_ref[...])

    ri = lax.broadcasted_iota(jnp.int32, (L, L), 0)
    ci = lax.broadcasted_iota(jnp.int32, (L, L), 1)
    tril = ri >= ci
    eye = ri == ci
    tri_bf = jnp.where(tril, 1.0, 0.0).astype(BF16)
    dti = (ri - ci).astype(F32)
    rowf = lax.broadcasted_iota(jnp.int32, (L, 1), 0).astype(F32)
    ret_dec, ret_rowdec, ret_kdec, ret_cdec = [], [], [], []
    for h in range(HEADS):
        lg = float(np.log1p(-(np.float32(2.0) ** np.float32(-5.0 - h))).astype(np.float32))
        ret_dec.append(jnp.exp(jnp.where(tril, dti * lg, NEG_BIG)))
        ret_rowdec.append(jnp.exp((rowf + 1.0) * lg))
        ret_kdec.append(jnp.exp((L - 1.0 - rowf) * lg))
        ret_cdec.append(float(np.exp(np.float32(L) * np.float32(lg))))
    lane256 = lax.broadcasted_iota(jnp.int32, (L, S_GROUP_LANES), 1)
    lane_in_seg = lane256 % QK
    head_of_lane = lane256 // S_HD
    row16 = lax.broadcasted_iota(jnp.int32, (H_BLOCK, 1), 0)

    lbp = lb_ref[...]
    lbe = jnp.exp(lbp - jnp.max(lbp, axis=0, keepdims=True))
    lbs = lbe / jnp.sum(lbe, axis=0, keepdims=True)
    lbc = lbs[0:1]
    for i in range(1, layer + 1):
        lbc = lbc + lbs[i:i + 1]
    lbr = lbc - lbs[0:1]

    def rot(v, cosf, sinf):
        sw = jnp.where(lane_in_seg < QK // 2,
                       pltpu.roll(v, S_GROUP_LANES - QK // 2, 1), pltpu.roll(v, QK // 2, 1))
        return v * cosf + sw * sinf

    def hgrn_inputs(rsel):
        def col(name):
            off, w = SEC[name]
            return proj_ref[rsel, off:off + w]
        hf = col("hf")
        zf = jnp.exp(-jnp.abs(hf))
        rf = 1.0 / (1.0 + zf)
        zr = zf * rf
        f_pos = hf >= 0.0
        kk = (1.0 - lbr) * jnp.where(f_pos, zr, rf)
        lfh = jnp.log(lbr + (1.0 - lbr) * jnp.where(f_pos, rf, zr))
        return col("hq") * (HD ** -0.5), kk, lfh, col("hi"), _silu(col("hg"))

    def out_proj_piece(r_lo, r_hi, c_lo, k_lo=0, k_hi=MIX_WIDTH):
        def emit():
            cols = slice(c_lo, c_lo + MXU_COLS)
            part = _dot(mix_ref[r_lo:r_hi, k_lo:k_hi], wout_ref[k_lo:k_hi, cols])
            if k_lo == 0:
                y_ref[0, r_lo:r_hi, cols] = part + x_ref[0, r_lo:r_hi, cols]
            else:
                y_ref[0, r_lo:r_hi, cols] += part
        return emit

    def finish_rows(r_lo, r_hi):
        if last:
            y_ref[0, r_lo:r_hi, :] = _rms(y_ref[0, r_lo:r_hi, :], D_MODEL) * fnw_ref[...]

    def chunk(c, hgrn_state, extra_mxu_work):
        r0 = c * L
        rows = slice(r0, r0 + L)
        s = 0 if carry else c

        def sec(name):
            off, w = SEC[name]
            return proj_ref[rows, off:off + w]

        def proj_piece(o, w):
            def emit():
                stage_ref[:, o:o + w] = _dot(hnn_ref[...], win_ref[:, o:o + w])
            return emit

        pieces = ([(0, proj_piece(o, min(MXU_COLS, col_hi[c] - o))) for o in range(col_lo[c], col_hi[c], MXU_COLS)]
                  if pipelined else [])
        pieces = pieces + list(extra_mxu_work)
        n_slots = 2 * HEADS + S_GROUPS
        per_slot = -(-len(pieces) // n_slots)

        def next_proj_piece(stage=0, limit=None):
            issued = 0
            for entry in list(pieces):
                if issued == (per_slot if limit is None else limit):
                    break
                if entry[0] <= stage:
                    pieces.remove(entry)
                    entry[1]()
                    issued += 1

        g = sec("mif") + gb_ref[...]
        ig = g
        lf = _log_sigmoid(pltpu.roll(g, LANES - HEADS, 1))
        b = _cumsum_rows(lf, L)
        a = ig - b
        m_prev = mm_o[s]
        big_a = jnp.maximum(m_prev, _cummax_rows(a, L))
        m_t = b + big_a
        wi = jnp.exp(m_prev - big_a)
        em = jnp.exp(-m_t)
        a_last = big_a[L - 1:L]
        wk = jnp.exp(a - a_last)
        wc = jnp.exp(m_prev - a_last)
        mm_o[s] = m_t[L - 1:L]
        q_all = sec("mq") * (QK ** -0.5)
        k_all = sec("mk")
        v_all = sec("mv")
        o_all = sec("mo")
        z_all = sec("mz")

        def mlstm_head(h):
            ks = slice(h * QK, (h + 1) * QK)
            vs = slice(h * HD, (h + 1) * HD)
            qh, kh = q_all[:, ks], k_all[:, ks]
            qb, kb, vb = qh.astype(BF16), kh.astype(BF16), v_all[:, vs].astype(BF16)
            a_row = _col_to_row(a[:, h:h + 1], eye)
            e = jnp.exp(jnp.where(tril, a_row - big_a[:, h:h + 1], NEG_BIG))
            sc = _dot_nt(qb, kb) * e
            c_prev = mc_o[s, h]
            n_prev = mn_o[s, pl.ds(h, 1), :]
            wi_c = wi[:, h:h + 1]
            num = _dot(sc.astype(BF16), vb) + wi_c * _dot(qb, c_prev.astype(BF16))
            den = (jnp.sum(sc, axis=-1, keepdims=True)
                   + wi_c * jnp.sum(qh * n_prev, axis=-1, keepdims=True))
            hh = num / jnp.maximum(jnp.abs(den), em[:, h:h + 1])
            kw = kh * wk[:, h:h + 1]
            wc_h = wc[:, h:h + 1]
            mc_o[s, h] = wc_h * c_prev + _dot_tn(kw.astype(BF16), vb)
            mn_o[s, pl.ds(h, 1), :] = wc_h * n_prev + jnp.sum(kw, axis=0, keepdims=True)
            out = (_rms(hh, HD) * mnw_ref[:, vs]) * _sigmoid(o_all[:, vs]) * _silu(z_all[:, vs])
            mix_ref[rows, h * HD:(h + 1) * HD] = out.astype(BF16)

        cosf = cos_ref[rows, :]
        sinf = sin_ref[rows, :]
        rq = rot(sec("rq"), cosf, sinf) * (QK ** -0.5)
        rk = rot(sec("rk"), cosf, sinf)
        rv = sec("rv")
        rg = sec("rg")

        def ret_head(h):
            ks = slice(h * QK, (h + 1) * QK)
            vs = slice(h * HD, (h + 1) * HD)
            kh = rk[:, ks]
            qb, kb, vb = rq[:, ks].astype(BF16), kh.astype(BF16), rv[:, vs].astype(BF16)
            sc = _dot_nt(qb, kb) * ret_dec[h]
            s_prev = rs_o[s, h]
            o = _dot(sc.astype(BF16), vb) + ret_rowdec[h] * _dot(qb, s_prev.astype(BF16))
            rs_o[s, h] = ret_cdec[h] * s_prev + _dot_tn((kh * ret_kdec[h]).astype(BF16), vb)
            cen = o - jnp.sum(o, axis=-1, keepdims=True) * (1.0 / HD)
            out = _rms(cen, HD) * _silu(rg[:, vs])
            mix_ref[rows, GROUP_WIDTH + h * HD:GROUP_WIDTH + (h + 1) * HD] = out.astype(BF16)

        for h in range(HEADS):
            mlstm_head(h)
            ret_head(h)
            next_proj_piece()

        ext_ref[SUBLANES - (CONV_W - 1):SUBLANES, :] = sb_o[s]
        ext_ref[SUBLANES:SUBLANES + L, :] = sec("sxbc")
        acc = cb_ref[...] + cw_ref[CONV_W - 1:CONV_W, :] * ext_ref[SUBLANES:SUBLANES + L, :]
        for j in range(CONV_W - 1):
            st = SUBLANES - (CONV_W - 1) + j
            acc = acc + cw_ref[j:j + 1, :] * ext_ref[st:st + L, :]
        sb_o[s] = ext_ref[SUBLANES + L - (CONV_W - 1):SUBLANES + L, :]
        xbc = _silu(acc)
        xs = xbc[:, :GROUP_WIDTH]
        bs = xbc[:, GROUP_WIDTH:GROUP_WIDTH + S_GROUPS * S_STATE]
        cs = xbc[:, GROUP_WIDTH + S_GROUPS * S_STATE:]
        dtc = _softplus(sec("sdt") + dtb_ref[...])
        la = dtc * (-jnp.exp(alog_ref[...]))
        bcum = _cumsum_rows(la, L)
        b_last = bcum[L - 1:L]
        xv = xs * _expand_heads64(dtc)
        xw = xv * _expand_heads64(jnp.exp(b_last - bcum))
        eb_e = _expand_heads64(jnp.exp(bcum))
        ebl_e = _expand_heads64(jnp.exp(b_last))
        ys = []
        for gi in range(S_GROUPS):
            gl = slice(gi * S_GROUP_LANES, (gi + 1) * S_GROUP_LANES)
            cq = cs[:, gi * S_STATE:(gi + 1) * S_STATE].astype(BF16)
            bk = bs[:, gi * S_STATE:(gi + 1) * S_STATE].astype(BF16)
            h_prev = sh_o[s, gi]
            y_g = eb_e[:, gl] * _dot(cq, h_prev.astype(BF16))
            xv_g = xv[:, gl]
            cbm = _dot_nt(cq, bk)
            for hh in range(S_HEADS // S_GROUPS):
                h = gi * (S_HEADS // S_GROUPS) + hh
                b_col = bcum[:, h:h + 1]
                dec = jnp.exp(jnp.where(tril, b_col - _col_to_row(b_col, eye), NEG_BIG))
                xm = jnp.where(head_of_lane == hh, xv_g, 0.0).astype(BF16)
                y_g = y_g + _dot((cbm * dec).astype(BF16), xm)
            sh_o[s, gi] = ebl_e[:, gl] * h_prev + _dot_tn(bk, xw[:, gl].astype(BF16))
            ys.append(y_g)
            next_proj_piece(stage=1)
        y_s = jnp.concatenate(ys, axis=1) + dsk_ref[...] * xs
        y_s = y_s * _silu(sec("sz"))
        for gi in range(S_GROUPS):
            gl = slice(gi * S_GROUP_LANES, (gi + 1) * S_GROUP_LANES)
            out = _rms(y_s[:, gl], S_GROUP_LANES) * snw_ref[:, gl]
            mix_ref[rows, 2 * GROUP_WIDTH + gi * S_GROUP_LANES:
                    2 * GROUP_WIDTH + (gi + 1) * S_GROUP_LANES] = out.astype(BF16)

        hq, kk, lfh, hv, hg_gate = hgrn_inputs(rows)
        bch = _cumsum_mxu(lfh, tri_bf)
        anchor = bch[L // 2 - 1:L // 2]
        blast = bch[L - 1:L]
        dq = bch - anchor
        worst = jnp.max(jnp.max(jnp.abs(dq), axis=1, keepdims=True), axis=0, keepdims=True)
        unsafe = jnp.logical_not(worst[0, 0] <= HGRN_SAFE_EXP)
        qe = hq * jnp.exp(dq)
        ke = kk * jnp.exp(-dq)
        qs = qe * jnp.exp(anchor)
        kend = ke * jnp.exp(blast - anchor)
        ebl = jnp.exp(blast)
        new_state = []
        for h in range(HEADS):
            vs = slice(h * HD, (h + 1) * HD)
            vb = hv[:, vs].astype(BF16)
            st_prev = hgrn_state[h]
            sc = jnp.where(tril, _dot_nt(qe[:, vs].astype(BF16), ke[:, vs].astype(BF16)), 0.0)
            o = _dot(sc.astype(BF16), vb) + _dot_nt(qs[:, vs].astype(BF16), st_prev.astype(BF16))
            new_state.append(st_prev * ebl[:, vs] + _dot_tn(vb, kend[:, vs].astype(BF16)))
            out = _rms(o, HD) * hnw_ref[:, vs] * hg_gate[:, vs]
            mix_ref[rows, 3 * GROUP_WIDTH + h * HD:3 * GROUP_WIDTH + (h + 1) * HD] = out.astype(BF16)
            next_proj_piece(stage=2)
        next_proj_piece(stage=3, limit=len(pieces))
        return new_state, unsafe

    half = tm // 2
    early_out = half % L == 0 and nch >= 2 and D_MODEL % MXU_COLS == 0
    out_cols = range(0, D_MODEL, MXU_COLS)
    first_half = [out_proj_piece(0, half, c_lo) for c_lo in out_cols] if early_out else []
    late_chunks = list(range(half // L, nch)) if early_out else []
    any_unsafe = None
    state = None
    for c in range(nch):
        if state is None or not carry:
            state = [hs_o[0 if carry else c, h] for h in range(HEADS)]
        extra = []
        if c in late_chunks:
            k = late_chunks.index(c)
            per = -(-len(first_half) // len(late_chunks))
            extra = [(0, emit) for emit in first_half[k * per:(k + 1) * per]]
        state, unsafe = chunk(c, state, extra)
        any_unsafe = unsafe if any_unsafe is None else jnp.logical_or(any_unsafe, unsafe)
        if not carry or c == nch - 1:
            for h in range(HEADS):
                hnew_ref[0 if carry else c, h] = state[h]
        if pipelined:
            ready = [(c, q) for q in range(c + 1)] + [(r, c) for r in range(c)]
            for r, q in ready:
                lo, hi = col_lo[q], min(col_hi[q], hgrn_col0)
                if lo < hi:
                    proj_ref[r * L:(r + 1) * L, lo:hi] = stage_ref[r * L:(r + 1) * L, lo:hi]

    @pl.when(any_unsafe)
    def _exact_hgrn():
        hnew_ref[...] = hs_o[...]

        def block(i, carry_val):
            r16 = pl.multiple_of(i * H_BLOCK, H_BLOCK)
            brow = pl.ds(r16, H_BLOCK)
            si = 0 if carry else (i if L == H_BLOCK else i // (L // H_BLOCK))
            hq, kk, lfh, hv, hg_gate = hgrn_inputs(brow)
            b16 = _cumsum_rows(lfh, H_BLOCK)
            bl = b16[H_BLOCK - 1:H_BLOCK]
            qe = hq * jnp.exp(b16)
            kend = kk * jnp.exp(bl - b16)
            ebl = jnp.exp(bl)
            for h in range(HEADS):
                vs = slice(h * HD, (h + 1) * HD)
                st = hnew_ref[si, h]
                bh, qh, kh, vh = b16[:, vs], hq[:, vs], kk[:, vs], hv[:, vs]
                o = _dot_nt(qe[:, vs].astype(BF16), st.astype(BF16))
                for sp in range(H_BLOCK):
                    w = jnp.exp(bh - bh[sp:sp + 1]) * qh * kh[sp:sp + 1]
                    scol = jnp.where(row16 >= sp, jnp.sum(w, axis=-1, keepdims=True), 0.0)
                    o = o + scol * vh[sp:sp + 1]
                hnew_ref[si, h] = st * ebl[:, vs] + _dot_tn(vh.astype(BF16), kend[:, vs].astype(BF16))
                out = _rms(o, HD) * hnw_ref[:, vs] * hg_gate[:, vs]
                mix_ref[brow, 3 * GROUP_WIDTH + h * HD:3 * GROUP_WIDTH + (h + 1) * HD] = out.astype(BF16)
            return carry_val

        lax.fori_loop(0, tm // H_BLOCK, block, 0)
        for emit in first_half:
            emit()

    hs_o[...] = hnew_ref[...]

    if pipelined:
        proj_ref[:, hgrn_col0:] = stage_ref[:, hgrn_col0:]
    if early_out:
        finish_rows(0, half)
        for c_lo in out_cols:
            out_proj_piece(half, tm, c_lo)()
        finish_rows(half, tm)
    else:
        y = _dot(mix_ref[...], wout_ref[...]) + x
        if last:
            y = _rms(y, D_MODEL) * fnw_ref[...]
        y_ref[0] = y

    @pl.when(t_id == nt - 1)
    def _fin():
        for s in range(sb):
            for h in range(HEADS):
                hs_o[s, h] = hs_o[s, h].T


N_STATES = 7
N_WEIGHTS = 14


def _layer_entry(*refs, n_alias, **static):
    n_in = 4 + N_STATES + N_WEIGHTS
    _layer_kernel(*refs[:n_in], *refs[n_in + n_alias:], **static)


def _layer_call(x, cosf, sinf, states, weights, prev_states_out, *, layer, depth, last, carry, tm, lc):
    bsz, t_len, _ = x.shape
    nt = t_len // tm
    state_shapes = [a.shape[1:] if st else a.shape for a, st in states]
    sb = 1 if carry else state_shapes[0][0]
    grid = (bsz, nt)
    n_steps = bsz * nt
    pipelined = n_steps > 1

    def batched_spec(shape, stacked):
        blk = ((None,) if stacked else ()) + (sb,) + tuple(shape[1:])
        lead = (layer,) if stacked else ()
        zeros = (0,) * (len(shape) - 1)
        return pl.BlockSpec(blk, lambda b, t: lead + (b,) + zeros)

    def const_spec(arr, stacked):
        shape = arr.shape[1:] if stacked else arr.shape
        blk = ((None,) if stacked else ()) + tuple(shape)
        idx = ((layer,) if stacked else ()) + (0,) * len(shape)
        return pl.BlockSpec(blk, lambda b, t: idx, pipeline_mode=pl.Buffered(1))

    def next_tile(b, t):
        nxt = jnp.minimum(b * nt + t + 1, n_steps - 1)
        return (nxt // nt, nxt % nt, 0)

    n_alias = 0 if prev_states_out is None else len(prev_states_out)
    in_specs = ([pl.BlockSpec((1, tm, D_MODEL), lambda b, t: (b, t, 0)),
                 pl.BlockSpec((1, tm, D_MODEL), next_tile),
                 pl.BlockSpec((tm, S_GROUP_LANES), lambda b, t: (t, 0)),
                 pl.BlockSpec((tm, S_GROUP_LANES), lambda b, t: (t, 0))]
                + [batched_spec(shp, st) for shp, (_, st) in zip(state_shapes, states)]
                + [const_spec(w, st) for w, st in weights]
                + [pl.BlockSpec(memory_space=pl.ANY)] * n_alias)
    out_shape = ([jax.ShapeDtypeStruct(x.shape, F32)]
                 + [jax.ShapeDtypeStruct((depth,) + tuple(shp), F32) for shp in state_shapes])
    out_specs = ([pl.BlockSpec((1, tm, D_MODEL), lambda b, t: (b, t, 0))]
                 + [batched_spec(shp, True) for shp in state_shapes])
    n_in = 4 + N_STATES + N_WEIGHTS
    kern = functools.partial(_layer_entry, n_alias=n_alias, layer=layer, last=last, carry=carry,
                             tm=tm, lc=lc, sb=sb, nt=nt, pipelined=pipelined)
    operands = ([x, x, cosf, sinf] + [a for a, _ in states] + [w for w, _ in weights]
                + list(prev_states_out or []))
    outs = pl.pallas_call(
        kern,
        grid=grid,
        in_specs=in_specs,
        out_specs=out_specs,
        out_shape=out_shape,
        input_output_aliases={n_in + i: 1 + i for i in range(n_alias)},
        scratch_shapes=[pltpu.VMEM((tm, PROJ_PAD), F32),
                        pltpu.VMEM((tm if pipelined else SUBLANES, PROJ_PAD), F32),
                        pltpu.VMEM((tm, D_MODEL), BF16),
                        pltpu.VMEM((tm, MIX_WIDTH), BF16),
                        pltpu.VMEM((lc + SUBLANES, S_CONV_DIM), F32),
                        pltpu.VMEM((sb, HEADS, HD, HD), F32)],
        compiler_params=pltpu.CompilerParams(
            dimension_semantics=("arbitrary", "arbitrary"),
            vmem_limit_bytes=VMEM_LIMIT_BYTES),
        name=f"hybrid_layer{layer}_{'prompt' if carry else 'sample'}",
    )(*operands)
    return outs[0], list(outs[1:])


def _pack_w_in(w):
    parts, off = [], 0
    for width in IN_SIZES:
        p = w[:, off:off + width]
        if _padded(width) != width:
            p = jnp.pad(p, ((0, 0), (0, _padded(width) - width)))
        parts.append(p)
        off += width
    parts.append(jnp.zeros((w.shape[0], PROJ_PAD - PROJ_COLS), w.dtype))
    return jnp.concatenate(parts, axis=1).astype(BF16)


def _pad_lanes(v):
    return jnp.pad(v.astype(F32), (0, LANES - v.shape[0]))[None, :]


def _rope_tables(pos):
    half = QK // 2
    freqs = np.float64(ROPE_BASE) ** (-np.arange(half, dtype=np.float64) / half)
    ang = np.asarray(pos, np.float64)[:, None] * freqs[None, :]
    cos, sin = np.cos(ang), np.sin(ang)
    cosf = np.tile(np.concatenate([cos, cos], axis=1), (1, HEADS))
    sinf = np.tile(np.concatenate([-sin, sin], axis=1), (1, HEADS))
    return jnp.asarray(cosf, F32), jnp.asarray(sinf, F32)


def _ssd_state_to_lanes(h):
    lead = h.shape[:-3]
    hpg = S_HEADS // S_GROUPS
    n = len(lead)
    perm = tuple(range(n)) + (n, n + 2, n + 1, n + 3)
    return h.reshape(lead + (S_GROUPS, hpg, S_STATE, S_HD)).transpose(perm).reshape(
        lead + (S_GROUPS, S_STATE, S_GROUP_LANES))


def _ssd_state_from_lanes(h):
    lead = h.shape[:-3]
    hpg = S_HEADS // S_GROUPS
    n = len(lead)
    perm = tuple(range(n)) + (n, n + 2, n + 1, n + 3)
    return h.reshape(lead + (S_GROUPS, S_STATE, hpg, S_HD)).transpose(perm).reshape(
        lead + (S_HEADS, S_STATE, S_HD))


def _run_group(x, cosf, sinf, init_states, win_packed, weights, *, carry, tm, lc):
    depth = len(win_packed)
    st = [jnp.zeros((depth,) + tuple(a.shape[1:] if stacked else a.shape), F32)
          for a, stacked in init_states]
    for l in range(depth):
        layer_weights = [weights[0], (win_packed[l], False)] + weights[1:]
        x, st = _layer_call(x, cosf, sinf, init_states, layer_weights, st, layer=l, depth=depth,
                            last=(l == depth - 1), carry=carry, tm=tm, lc=lc)
    mc, mn, mm, rs, sh, sbuf, hs = st
    return x, [mc, mn, mm[:, :, 0, :HEADS], rs, _ssd_state_from_lanes(sh), sbuf, hs]


def kernel(x_prompt, x_sample, state_mlstm_c, state_mlstm_n, state_mlstm_m, state_ret, state_ssd,
           cache_ssd_conv, state_hgrn, norm_w, w_in, mlstm_gate_b, mlstm_norm_w, ssd_conv_w,
           ssd_conv_b, ssd_dt_bias, ssd_a_log, ssd_d, ssd_norm_w, hgrn_lower_bounds, hgrn_norm_w,
           w_out, final_norm_w):
    depth = w_in.shape[0]

    def rows(p):
        return p.astype(F32)[:, None, :]

    def padded_rows(p):
        return rows(jnp.pad(p.astype(F32), ((0, 0), (0, LANES - p.shape[1]))))

    win_packed = [_pack_w_in(w_in[l]) for l in range(depth)]
    weights = [
        (rows(norm_w), True),
        (padded_rows(mlstm_gate_b), True),
        (rows(mlstm_norm_w), True),
        (ssd_conv_w.astype(F32), True),
        (rows(ssd_conv_b), True),
        (padded_rows(ssd_dt_bias), True),
        (padded_rows(ssd_a_log), True),
        (rows(jnp.repeat(ssd_d.astype(F32), S_HD, axis=1)), True),
        (rows(ssd_norm_w), True),
        (hgrn_lower_bounds.astype(F32), False),
        (rows(hgrn_norm_w), True),
        (w_out.astype(BF16), True),
        (final_norm_w.astype(F32)[None, :], False),
    ]

    bp, tp, _ = x_prompt.shape
    lc_p = CHUNK if tp % CHUNK == 0 else tp
    tm_p = 256 if tp % 256 == 0 else lc_p
    zero_c = jnp.zeros((bp, HEADS, QK, HD), F32)
    empty = [(zero_c, False), (jnp.zeros((bp, HEADS, QK), F32), False),
             (jnp.full((bp, 1, LANES), NEG_INIT, F32), False), (zero_c, False),
             (jnp.zeros((bp, S_GROUPS, S_STATE, S_GROUP_LANES), F32), False),
             (jnp.zeros((bp, CONV_W - 1, S_CONV_DIM), F32), False),
             (jnp.zeros((bp, HEADS, HD, HD), F32), False)]
    cos_p, sin_p = _rope_tables(np.arange(tp))
    y_prompt, p_st = _run_group(x_prompt, cos_p, sin_p, empty, win_packed, weights,
                                carry=True, tm=tm_p, lc=lc_p)

    bs_, ts, _ = x_sample.shape
    lc_s = CHUNK if ts % CHUNK == 0 else ts
    mm = jnp.pad(state_mlstm_m.astype(F32), ((0, 0), (0, 0), (0, LANES - HEADS)))[:, :, None, :]
    carried = [(state_mlstm_c.astype(F32), True), (state_mlstm_n.astype(F32), True), (mm, True),
               (state_ret.astype(F32), True), (_ssd_state_to_lanes(state_ssd.astype(F32)), True),
               (cache_ssd_conv.astype(F32), True), (state_hgrn.astype(F32), True)]
    cos_s, sin_s = _rope_tables(np.tile(PAST_LEN + np.arange(ts), bs_))
    y_sample, s_st = _run_group(x_sample.reshape(1, bs_ * ts, D_MODEL), cos_s, sin_s, carried,
                                win_packed, weights, carry=False, tm=bs_ * ts, lc=lc_s)
    y_sample = y_sample.reshape(bs_, ts, D_MODEL)

    return (y_prompt, y_sample, *p_st, *s_st)
```

```python
import functools

import numpy as np
import jax
import jax.numpy as jnp
from jax import lax
from jax.experimental import pallas as pl
from jax.experimental.pallas import tpu as pltpu

F32 = jnp.float32
BF16 = jnp.bfloat16

D_MODEL = 1024
GROUP_WIDTH = 512
MIX_WIDTH = 4 * GROUP_WIDTH
EPS = 1e-6
NEG_INIT = -1e30
NEG_BIG = -1e30
CHUNK = 64
PAST_LEN = 1024
ROPE_BASE = 10000.0
HEADS = 4
QK = 64
HD = 128
S_HEADS = 8
S_HD = 64
S_GROUPS = 2
S_STATE = 128
S_GROUP_LANES = (S_HEADS // S_GROUPS) * S_HD
CONV_W = 4
S_CONV_DIM = 1024
H_BLOCK = 16
MXU_COLS = 256
HGRN_SAFE_EXP = 60.0
LANES = 128
SUBLANES = 8
VMEM_LIMIT_BYTES = 56 * 1024 * 1024

IN_SIZES = (256, 256, 512, 512, 512, 8, 256, 256, 512, 512, 512, 1024, 8, 512, 512, 512, 512)
SEC_NAMES = ("mq", "mk", "mv", "mo", "mz", "mif", "rq", "rk", "rv", "rg", "sz", "sxbc", "sdt",
             "hq", "hf", "hi", "hg")


def _padded(width):
    return -(-width // LANES) * LANES


SEC = {}
_off = 0
for _n, _w in zip(SEC_NAMES, IN_SIZES):
    SEC[_n] = (_off, _padded(_w))
    _off += _padded(_w)
PROJ_COLS = _off
PROJ_PAD = -(-PROJ_COLS // MXU_COLS) * MXU_COLS


def _dot(a, b):
    return jnp.dot(a, b, preferred_element_type=F32)


def _dot_nt(a, b):
    return lax.dot_general(a, b, (((1,), (1,)), ((), ())), preferred_element_type=F32)


def _dot_tn(a, b):
    return lax.dot_general(a, b, (((0,), (0,)), ((), ())), preferred_element_type=F32)


def _sigmoid(x):
    return 0.5 * jnp.tanh(0.5 * x) + 0.5


def _silu(x):
    return x * _sigmoid(x)


def _softplus(x):
    return jnp.maximum(x, 0.0) + jnp.log1p(jnp.exp(-jnp.abs(x)))


def _log_sigmoid(x):
    return jnp.minimum(x, 0.0) - jnp.log1p(jnp.exp(-jnp.abs(x)))


def _cumsum_rows(x, n):
    rid = lax.broadcasted_iota(jnp.int32, x.shape, 0)
    s = 1
    while s < n:
        x = x + jnp.where(rid >= s, pltpu.roll(x, s, 0), 0.0)
        s *= 2
    return x


def _cumsum_mxu(x, tri):
    hi = x.astype(BF16)
    r1 = x - hi.astype(F32)
    mid = r1.astype(BF16)
    lo = (r1 - mid.astype(F32)).astype(BF16)
    return _dot(tri, hi) + _dot(tri, mid) + _dot(tri, lo)


def _cummax_rows(x, n):
    rid = lax.broadcasted_iota(jnp.int32, x.shape, 0)
    s = 1
    while s < n:
        x = jnp.maximum(x, jnp.where(rid >= s, pltpu.roll(x, s, 0), x))
        s *= 2
    return x


def _col_to_row(col, eye):
    return jnp.sum(jnp.where(eye, col, 0.0), axis=0, keepdims=True)


def _expand_heads64(c):
    lane = lax.broadcasted_iota(jnp.int32, (c.shape[0], LANES), 1)
    parts = [jnp.where(lane < S_HD, c[:, 2 * j:2 * j + 1], c[:, 2 * j + 1:2 * j + 2])
             for j in range(S_HEADS // 2)]
    return jnp.concatenate(parts, axis=1)


def _rms(h, width):
    return h * lax.rsqrt(jnp.sum(h * h, axis=-1, keepdims=True) * (1.0 / width) + EPS)


def _norm_rows(x, w):
    return (x * lax.rsqrt(jnp.sum(x * x, axis=-1, keepdims=True) * (1.0 / D_MODEL) + EPS)
            * w).astype(BF16)


def _layer_kernel(x_ref, xn_ref, cos_ref, sin_ref, mc_i, mn_i, mm_i, rs_i, sh_i, sb_i, hs_i,
                  nw_ref, win_ref, gb_ref, mnw_ref, cw_ref, cb_ref, dtb_ref, alog_ref, dsk_ref,
                  snw_ref, lb_ref, hnw_ref, wout_ref, fnw_ref,
                  y_ref, mc_o, mn_o, mm_o, rs_o, sh_o, sb_o, hs_o,
                  proj_ref, stage_ref, hnn_ref, mix_ref, ext_ref, hnew_ref,
                  *, layer, last, carry, tm, lc, sb, nt, pipelined):
    L = lc
    nch = tm // lc
    t_id = pl.program_id(1)
    if pipelined:
        step = pl.program_id(0) * nt + t_id
    n_pieces = -(-PROJ_PAD // MXU_COLS)
    n_out = D_MODEL // MXU_COLS if (tm // 2) % L == 0 and nch >= 2 else 0
    late = [c for c in range(nch) if c * L >= tm // 2] if n_out else []
    load = n_pieces + n_out
    piece_hi, out_cum = [], 0
    for c in range(nch):
        out_cum += -(-n_out // len(late)) if c in late else 0
        cum = (c + 1) * load // nch - min(out_cum, n_out)
        piece_hi.append(PROJ_PAD if c == nch - 1 else
                        min(max(cum, piece_hi[-1] // MXU_COLS if piece_hi else 0), n_pieces) * MXU_COLS)
    col_lo, col_hi = [0] + piece_hi[:-1], piece_hi
    hgrn_col0 = SEC["hq"][0]

    @pl.when(t_id == 0)
    def _init():
        mc_o[...] = mc_i[...]
        mn_o[...] = mn_i[...]
        mm_o[...] = mm_i[...]
        rs_o[...] = rs_i[...]
        sh_o[...] = sh_i[...]
        sb_o[...] = sb_i[...]
        for s in range(sb):
            for h in range(HEADS):
                hs_o[s, h] = hs_i[s, h].T

    x = x_ref[0]
    if pipelined:
        @pl.when(step == 0)
        def _first_proj():
            proj_ref[...] = _dot(_norm_rows(x, nw_ref[...]), win_ref[...])
        hnn_ref[...] = _norm_rows(xn_ref[0], nw_ref[...])
    else:
        proj_ref[...] = _dot(_norm_rows(x, nw_ref[...]), win_ref[...])

    ri = lax.broadcasted_iota(jnp.int32, (L, L), 0)
    ci = lax.broadcasted_iota(jnp.int32, (L, L), 1)
    tril = ri >= ci
    eye = ri == ci
    tri_bf = jnp.where(tril, 1.0, 0.0).astype(BF16)
    dti = (ri - ci).astype(F32)
    rowf = lax.broadcasted_iota(jnp.int32, (L, 1), 0).astype(F32)
    ret_dec, ret_rowdec, ret_kdec, ret_cdec = [], [], [], []
    for h in range(HEADS):
        lg = float(np.log1p(-(np.float32(2.0) ** np.float32(-5.0 - h))).astype(np.float32))
        ret_dec.append(jnp.exp(jnp.where(tril, dti * lg, NEG_BIG)))
        ret_rowdec.append(jnp.exp((rowf + 1.0) * lg))
        ret_kdec.append(jnp.exp((L - 1.0 - rowf) * lg))
        ret_cdec.append(float(np.exp(np.float32(L) * np.float32(lg))))
    lane256 = lax.broadcasted_iota(jnp.int32, (L, S_GROUP_LANES), 1)
    lane_in_seg = lane256 % QK
    head_of_lane = lane256 // S_HD
    row16 = lax.broadcasted_iota(jnp.int32, (H_BLOCK, 1), 0)

    lbp = lb_ref[...]
    lbe = jnp.exp(lbp - jnp.max(lbp, axis=0, keepdims=True))
    lbs = lbe / jnp.sum(lbe, axis=0, keepdims=True)
    lbc = lbs[0:1]
    for i in range(1, layer + 1):
        lbc = lbc + lbs[i:i + 1]
    lbr = lbc - lbs[0:1]

    def rot(v, cosf, sinf):
        sw = jnp.where(lane_in_seg < QK // 2,
                       pltpu.roll(v, S_GROUP_LANES - QK // 2, 1), pltpu.roll(v, QK // 2, 1))
        return v * cosf + sw * sinf

    def hgrn_inputs(rsel):
        def col(name):
            off, w = SEC[name]
            return proj_ref[rsel, off:off + w]
        hf = col("hf")
        zf = jnp.exp(-jnp.abs(hf))
        rf = 1.0 / (1.0 + zf)
        zr = zf * rf
        f_pos = hf >= 0.0
        kk = (1.0 - lbr) * jnp.where(f_pos, zr, rf)
        lfh = jnp.log(lbr + (1.0 - lbr) * jnp.where(f_pos, rf, zr))
        return col("hq") * (HD ** -0.5), kk, lfh, col("hi"), _silu(col("hg"))

    def out_proj_piece(r_lo, r_hi, c_lo, k_lo=0, k_hi=MIX_WIDTH):
        def emit():
            cols = slice(c_lo, c_lo + MXU_COLS)
            part = _dot(mix_ref[r_lo:r_hi, k_lo:k_hi], wout_ref[k_lo:k_hi, cols])
            if k_lo == 0:
                y_ref[0, r_lo:r_hi, cols] = part + x_ref[0, r_lo:r_hi, cols]
            else:
                y_ref[0, r_lo:r_hi, cols] += part
        return emit

    def finish_rows(r_lo, r_hi):
        if last:
            y_ref[0, r_lo:r_hi, :] = _rms(y_ref[0, r_lo:r_hi, :], D_MODEL) * fnw_ref[...]

    def chunk(c, hgrn_state, extra_mxu_work):
        r0 = c * L
        rows = slice(r0, r0 + L)
        s = 0 if carry else c

        def sec(name):
            off, w = SEC[name]
            return proj_ref[rows, off:off + w]

        def proj_piece(o, w):
            def emit():
                stage_ref[:, o:o + w] = _dot(hnn_ref[...], win_ref[:, o:o + w])
            return emit

        pieces = ([(0, proj_piece(o, min(MXU_COLS, col_hi[c] - o))) for o in range(col_lo[c], col_hi[c], MXU_COLS)]
                  if pipelined else [])
        pieces = pieces + list(extra_mxu_work)
        n_slots = 2 * HEADS + S_GROUPS
        per_slot = -(-len(pieces) // n_slots)

        def next_proj_piece(stage=0, limit=None):
            issued = 0
            for entry in list(pieces):
                if issued == (per_slot if limit is None else limit):
                    break
                if entry[0] <= stage:
                    pieces.remove(entry)
                    entry[1]()
                    issued += 1

        g = sec("mif") + gb_ref[...]
        ig = g
        lf = _log_sigmoid(pltpu.roll(g, LANES - HEADS, 1))
        b = _cumsum_rows(lf, L)
        a = ig - b
        m_prev = mm_o[s]
        big_a = jnp.maximum(m_prev, _cummax_rows(a, L))
        m_t = b + big_a
        wi = jnp.exp(m_prev - big_a)
        em = jnp.exp(-m_t)
        a_last = big_a[L - 1:L]
        wk = jnp.exp(a - a_last)
        wc = jnp.exp(m_prev - a_last)
        mm_o[s] = m_t[L - 1:L]
        q_all = sec("mq") * (QK ** -0.5)
        k_all = sec("mk")
        v_all = sec("mv")
        o_all = sec("mo")
        z_all = sec("mz")

        def mlstm_head(h):
            ks = slice(h * QK, (h + 1) * QK)
            vs = slice(h * HD, (h + 1) * HD)
            qh, kh = q_all[:, ks], k_all[:, ks]
            qb, kb, vb = qh.astype(BF16), kh.astype(BF16), v_all[:, vs].astype(BF16)
            a_row = _col_to_row(a[:, h:h + 1], eye)
            e = jnp.exp(jnp.where(tril, a_row - big_a[:, h:h + 1], NEG_BIG))
            sc = _dot_nt(qb, kb) * e
            c_prev = mc_o[s, h]
            n_prev = mn_o[s, pl.ds(h, 1), :]
            wi_c = wi[:, h:h + 1]
            num = _dot(sc.astype(BF16), vb) + wi_c * _dot(qb, c_prev.astype(BF16))
            den = (jnp.sum(sc, axis=-1, keepdims=True)
                   + wi_c * jnp.sum(qh * n_prev, axis=-1, keepdims=True))
            hh = num / jnp.maximum(jnp.abs(den), em[:, h:h + 1])
            kw = kh * wk[:, h:h + 1]
            wc_h = wc[:, h:h + 1]
            mc_o[s, h] = wc_h * c_prev + _dot_tn(kw.astype(BF16), vb)
            mn_o[s, pl.ds(h, 1), :] = wc_h * n_prev + jnp.sum(kw, axis=0, keepdims=True)
            out = (_rms(hh, HD) * mnw_ref[:, vs]) * _sigmoid(o_all[:, vs]) * _silu(z_all[:, vs])
            mix_ref[rows, h * HD:(h + 1) * HD] = out.astype(BF16)

        cosf = cos_ref[rows, :]
        sinf = sin_ref[rows, :]
        rq = rot(sec("rq"), cosf, sinf) * (QK ** -0.5)
        rk = rot(sec("rk"), cosf, sinf)
        rv = sec("rv")
        rg = sec("rg")

        def ret_head(h):
            ks = slice(h * QK, (h + 1) * QK)
            vs = slice(h * HD, (h + 1) * HD)
            kh = rk[:, ks]
            qb, kb, vb = rq[:, ks].astype(BF16), kh.astype(BF16), rv[:, vs].astype(BF16)
            sc = _dot_nt(qb, kb) * ret_dec[h]
            s_prev = rs_o[s, h]
            o = _dot(sc.astype(BF16), vb) + ret_rowdec[h] * _dot(qb, s_prev.astype(BF16))
            rs_o[s, h] = ret_cdec[h] * s_prev + _dot_tn((kh * ret_kdec[h]).astype(BF16), vb)
            cen = o - jnp.sum(o, axis=-1, keepdims=True) * (1.0 / HD)
            out = _rms(cen, HD) * _silu(rg[:, vs])
            mix_ref[rows, GROUP_WIDTH + h * HD:GROUP_WIDTH + (h + 1) * HD] = out.astype(BF16)

        for h in range(HEADS):
            mlstm_head(h)
            ret_head(h)
            next_proj_piece()

        ext_ref[SUBLANES - (CONV_W - 1):SUBLANES, :] = sb_o[s]
        ext_ref[SUBLANES:SUBLANES + L, :] = sec("sxbc")
        acc = cb_ref[...] + cw_ref[CONV_W - 1:CONV_W, :] * ext_ref[SUBLANES:SUBLANES + L, :]
        for j in range(CONV_W - 1):
            st = SUBLANES - (CONV_W - 1) + j
            acc = acc + cw_ref[j:j + 1, :] * ext_ref[st:st + L, :]
        sb_o[s] = ext_ref[SUBLANES + L - (CONV_W - 1):SUBLANES + L, :]
        xbc = _silu(acc)
        xs = xbc[:, :GROUP_WIDTH]
        bs = xbc[:, GROUP_WIDTH:GROUP_WIDTH + S_GROUPS * S_STATE]
        cs = xbc[:, GROUP_WIDTH + S_GROUPS * S_STATE:]
        dtc = _softplus(sec("sdt") + dtb_ref[...])
        la = dtc * (-jnp.exp(alog_ref[...]))
        bcum = _cumsum_rows(la, L)
        b_last = bcum[L - 1:L]
        xv = xs * _expand_heads64(dtc)
        xw = xv * _expand_heads64(jnp.exp(b_last - bcum))
        eb_e = _expand_heads64(jnp.exp(bcum))
        ebl_e = _expand_heads64(jnp.exp(b_last))
        ys = []
        for gi in range(S_GROUPS):
            gl = slice(gi * S_GROUP_LANES, (gi + 1) * S_GROUP_LANES)
            cq = cs[:, gi * S_STATE:(gi + 1) * S_STATE].astype(BF16)
            bk = bs[:, gi * S_STATE:(gi + 1) * S_STATE].astype(BF16)
            h_prev = sh_o[s, gi]
            y_g = eb_e[:, gl] * _dot(cq, h_prev.astype(BF16))
            xv_g = xv[:, gl]
            cbm = _dot_nt(cq, bk)
            for hh in range(S_HEADS // S_GROUPS):
                h = gi * (S_HEADS // S_GROUPS) + hh
                b_col = bcum[:, h:h + 1]
                dec = jnp.exp(jnp.where(tril, b_col - _col_to_row(b_col, eye), NEG_BIG))
                xm = jnp.where(head_of_lane == hh, xv_g, 0.0).astype(BF16)
                y_g = y_g + _dot((cbm * dec).astype(BF16), xm)
            sh_o[s, gi] = ebl_e[:, gl] * h_prev + _dot_tn(bk, xw[:, gl].astype(BF16))
            ys.append(y_g)
            next_proj_piece(stage=1)
        y_s = jnp.concatenate(ys, axis=1) + dsk_ref[...] * xs
        y_s = y_s * _silu(sec("sz"))
        for gi in range(S_GROUPS):
            gl = slice(gi * S_GROUP_LANES, (gi + 1) * S_GROUP_LANES)
            out = _rms(y_s[:, gl], S_GROUP_LANES) * snw_ref[:, gl]
            mix_ref[rows, 2 * GROUP_WIDTH + gi * S_GROUP_LANES:
                    2 * GROUP_WIDTH + (gi + 1) * S_GROUP_LANES] = out.astype(BF16)

        hq, kk, lfh, hv, hg_gate = hgrn_inputs(rows)
        bch = _cumsum_mxu(lfh, tri_bf)
        anchor = bch[L // 2 - 1:L // 2]
        blast = bch[L - 1:L]
        dq = bch - anchor
        worst = jnp.max(jnp.max(jnp.abs(dq), axis=1, keepdims=True), axis=0, keepdims=True)
        unsafe = jnp.logical_not(worst[0, 0] <= HGRN_SAFE_EXP)
        qe = hq * jnp.exp(dq)
        ke = kk * jnp.exp(-dq)
        qs = qe * jnp.exp(anchor)
        kend = ke * jnp.exp(blast - anchor)
        ebl = jnp.exp(blast)
        new_state = []
        for h in range(HEADS):
            vs = slice(h * HD, (h + 1) * HD)
            vb = hv[:, vs].astype(BF16)
            st_prev = hgrn_state[h]
            sc = jnp.where(tril, _dot_nt(qe[:, vs].astype(BF16), ke[:, vs].astype(BF16)), 0.0)
            o = _dot(sc.astype(BF16), vb) + _dot_nt(qs[:, vs].astype(BF16), st_prev.astype(BF16))
            new_state.append(st_prev * ebl[:, vs] + _dot_tn(vb, kend[:, vs].astype(BF16)))
            out = _rms(o, HD) * hnw_ref[:, vs] * hg_gate[:, vs]
            mix_ref[rows, 3 * GROUP_WIDTH + h * HD:3 * GROUP_WIDTH + (h + 1) * HD] = out.astype(BF16)
            next_proj_piece(stage=2)
        next_proj_piece(stage=3, limit=len(pieces))
        return new_state, unsafe

    half = tm // 2
    early_out = half % L == 0 and nch >= 2 and D_MODEL % MXU_COLS == 0
    out_cols = range(0, D_MODEL, MXU_COLS)
    first_half = [out_proj_piece(0, half, c_lo) for c_lo in out_cols] if early_out else []
    late_chunks = list(range(half // L, nch)) if early_out else []
    any_unsafe = None
    state = None
    for c in range(nch):
        if state is None or not carry:
            state = [hs_o[0 if carry else c, h] for h in range(HEADS)]
        extra = []
        if c in late_chunks:
            k = late_chunks.index(c)
            per = -(-len(first_half) // len(late_chunks))
            extra = [(0, emit) for emit in first_half[k * per:(k + 1) * per]]
        state, unsafe = chunk(c, state, extra)
        any_unsafe = unsafe if any_unsafe is None else jnp.logical_or(any_unsafe, unsafe)
        if not carry or c == nch - 1:
            for h in range(HEADS):
                hnew_ref[0 if carry else c, h] = state[h]
        if pipelined:
            ready = [(c, q) for q in range(c + 1)] + [(r, c) for r in range(c)]
            for r, q in ready:
                lo, hi = col_lo[q], min(col_hi[q], hgrn_col0)
                if lo < hi:
                    proj_ref[r * L:(r + 1) * L, lo:hi] = stage_ref[r * L:(r + 1) * L, lo:hi]

    @pl.when(any_unsafe)
    def _exact_hgrn():
        hnew_ref[...] = hs_o[...]

        def block(i, carry_val):
            r16 = pl.multiple_of(i * H_BLOCK, H_BLOCK)
            brow = pl.ds(r16, H_BLOCK)
            si = 0 if carry else (i if L == H_BLOCK else i // (L // H_BLOCK))
            hq, kk, lfh, hv, hg_gate = hgrn_inputs(brow)
            b16 = _cumsum_rows(lfh, H_BLOCK)
            bl = b16[H_BLOCK - 1:H_BLOCK]
            qe = hq * jnp.exp(b16)
            kend = kk * jnp.exp(bl - b16)
            ebl = jnp.exp(bl)
            for h in range(HEADS):
                vs = slice(h * HD, (h + 1) * HD)
                st = hnew_ref[si, h]
                bh, qh, kh, vh = b16[:, vs], hq[:, vs], kk[:, vs], hv[:, vs]
                o = _dot_nt(qe[:, vs].astype(BF16), st.astype(BF16))
                for sp in range(H_BLOCK):
                    w = jnp.exp(bh - bh[sp:sp + 1]) * qh * kh[sp:sp + 1]
                    scol = jnp.where(row16 >= sp, jnp.sum(w, axis=-1, keepdims=True), 0.0)
                    o = o + scol * vh[sp:sp + 1]
                hnew_ref[si, h] = st * ebl[:, vs] + _dot_tn(vh.astype(BF16), kend[:, vs].astype(BF16))
                out = _rms(o, HD) * hnw_ref[:, vs] * hg_gate[:, vs]
                mix_ref[brow, 3 * GROUP_WIDTH + h * HD:3 * GROUP_WIDTH + (h + 1) * HD] = out.astype(BF16)
            return carry_val

        lax.fori_loop(0, tm // H_BLOCK, block, 0)
        for emit in first_half:
            emit()

    hs_o[...] = hnew_ref[...]

    if pipelined:
        proj_ref[:, hgrn_col0:] = stage_ref[:, hgrn_col0:]
    if early_out:
        finish_rows(0, half)
        for c_lo in out_cols:
            out_proj_piece(half, tm, c_lo)()
        finish_rows(half, tm)
    else:
        y = _dot(mix_ref[...], wout_ref[...]) + x
        if last:
            y = _rms(y, D_MODEL) * fnw_ref[...]
        y_ref[0] = y

    @pl.when(t_id == nt - 1)
    def _fin():
        for s in range(sb):
            for h in range(HEADS):
                hs_o[s, h] = hs_o[s, h].T


def _const_spec(shape):
    nd = len(shape)
    return pl.BlockSpec(shape, lambda b, t, _nd=nd: (0,) * _nd, pipeline_mode=pl.Buffered(1))


def _layer_call(x, cosf, sinf, states, weights, *, layer, last, carry, tm, lc):
    bsz, t_len, _ = x.shape
    nt = t_len // tm
    sb = 1 if carry else states[0].shape[0]
    grid = (bsz, nt)
    n_steps = bsz * nt
    pipelined = n_steps > 1

    def state_spec(arr):
        blk = (sb,) + arr.shape[1:]
        nd = arr.ndim
        return pl.BlockSpec(blk, lambda b, t, _nd=nd: (b,) + (0,) * (_nd - 1))

    def next_tile(b, t):
        nxt = jnp.minimum(b * nt + t + 1, n_steps - 1)
        return (nxt // nt, nxt % nt, 0)

    in_specs = ([pl.BlockSpec((1, tm, D_MODEL), lambda b, t: (b, t, 0)),
                 pl.BlockSpec((1, tm, D_MODEL), next_tile),
                 pl.BlockSpec((tm, S_GROUP_LANES), lambda b, t: (t, 0)),
                 pl.BlockSpec((tm, S_GROUP_LANES), lambda b, t: (t, 0))]
                + [state_spec(a) for a in states]
                + [_const_spec(w.shape) for w in weights])
    out_shape = ([jax.ShapeDtypeStruct(x.shape, F32)]
                 + [jax.ShapeDtypeStruct(a.shape, F32) for a in states])
    out_specs = ([pl.BlockSpec((1, tm, D_MODEL), lambda b, t: (b, t, 0))]
                 + [state_spec(a) for a in states])
    kern = functools.partial(_layer_kernel, layer=layer, last=last, carry=carry, tm=tm, lc=lc,
                             sb=sb, nt=nt, pipelined=pipelined)
    outs = pl.pallas_call(
        kern,
        grid=grid,
        in_specs=in_specs,
        out_specs=out_specs,
        out_shape=out_shape,
        scratch_shapes=[pltpu.VMEM((tm, PROJ_PAD), F32),
                        pltpu.VMEM((tm if pipelined else SUBLANES, PROJ_PAD), F32),
                        pltpu.VMEM((tm, D_MODEL), BF16),
                        pltpu.VMEM((tm, MIX_WIDTH), BF16),
                        pltpu.VMEM((lc + SUBLANES, S_CONV_DIM), F32),
                        pltpu.VMEM((sb, HEADS, HD, HD), F32)],
        compiler_params=pltpu.CompilerParams(
            dimension_semantics=("arbitrary", "arbitrary"),
            vmem_limit_bytes=VMEM_LIMIT_BYTES),
        name=f"hybrid_layer{layer}_{'prompt' if carry else 'sample'}",
    )(x, x, cosf, sinf, *states, *weights)
    return outs[0], outs[1:]


def _pack_w_in(w):
    parts, off = [], 0
    for width in IN_SIZES:
        p = w[:, off:off + width]
        if _padded(width) != width:
            p = jnp.pad(p, ((0, 0), (0, _padded(width) - width)))
        parts.append(p)
        off += width
    parts.append(jnp.zeros((w.shape[0], PROJ_PAD - PROJ_COLS), w.dtype))
    return jnp.concatenate(parts, axis=1).astype(BF16)


def _pad_lanes(v):
    return jnp.pad(v.astype(F32), (0, LANES - v.shape[0]))[None, :]


def _rope_tables(pos):
    half = QK // 2
    freqs = np.float64(ROPE_BASE) ** (-np.arange(half, dtype=np.float64) / half)
    ang = np.asarray(pos, np.float64)[:, None] * freqs[None, :]
    cos, sin = np.cos(ang), np.sin(ang)
    cosf = np.tile(np.concatenate([cos, cos], axis=1), (1, HEADS))
    sinf = np.tile(np.concatenate([-sin, sin], axis=1), (1, HEADS))
    return jnp.asarray(cosf, F32), jnp.asarray(sinf, F32)


def _ssd_state_to_lanes(h):
    b = h.shape[0]
    hpg = S_HEADS // S_GROUPS
    return h.reshape(b, S_GROUPS, hpg, S_STATE, S_HD).transpose(0, 1, 3, 2, 4).reshape(
        b, S_GROUPS, S_STATE, S_GROUP_LANES)


def _ssd_state_from_lanes(h):
    b = h.shape[0]
    hpg = S_HEADS // S_GROUPS
    return h.reshape(b, S_GROUPS, S_STATE, hpg, S_HD).transpose(0, 1, 3, 2, 4).reshape(
        b, S_HEADS, S_STATE, S_HD)


def _run_group(x, cosf, sinf, init_states, layer_weights, *, carry, tm, lc):
    depth = len(layer_weights)
    per_layer = []
    for l in range(depth):
        x, st = _layer_call(x, cosf, sinf, init_states[l], layer_weights[l], layer=l,
                            last=(l == depth - 1), carry=carry, tm=tm, lc=lc)
        mc, mn, mm, rs, sh, sbuf, hs = st
        per_layer.append((mc, mn, mm[:, 0, :HEADS], rs, _ssd_state_from_lanes(sh), sbuf, hs))
    stacked = [jnp.stack([st[i] for st in per_layer], axis=0) for i in range(7)]
    return x, stacked


def kernel(x_prompt, x_sample, state_mlstm_c, state_mlstm_n, state_mlstm_m, state_ret, state_ssd,
           cache_ssd_conv, state_hgrn, norm_w, w_in, mlstm_gate_b, mlstm_norm_w, ssd_conv_w,
           ssd_conv_b, ssd_dt_bias, ssd_a_log, ssd_d, ssd_norm_w, hgrn_lower_bounds, hgrn_norm_w,
           w_out, final_norm_w):
    depth = w_in.shape[0]
    lb_all = hgrn_lower_bounds.astype(F32)
    layer_weights = []
    for l in range(depth):
        layer_weights.append((
            norm_w[l].astype(F32)[None, :],
            _pack_w_in(w_in[l]),
            _pad_lanes(mlstm_gate_b[l]),
            mlstm_norm_w[l].astype(F32)[None, :],
            ssd_conv_w[l].astype(F32),
            ssd_conv_b[l].astype(F32)[None, :],
            _pad_lanes(ssd_dt_bias[l]),
            _pad_lanes(ssd_a_log[l]),
            jnp.repeat(ssd_d[l].astype(F32), S_HD)[None, :],
            ssd_norm_w[l].astype(F32)[None, :],
            lb_all,
            hgrn_norm_w[l].astype(F32)[None, :],
            w_out[l].astype(BF16),
            final_norm_w.astype(F32)[None, :],
        ))

    bp, tp, _ = x_prompt.shape
    lc_p = CHUNK if tp % CHUNK == 0 else tp
    tm_p = 256 if tp % 256 == 0 else lc_p
    empty = (jnp.zeros((bp, HEADS, QK, HD), F32), jnp.zeros((bp, HEADS, QK), F32),
             jnp.full((bp, 1, LANES), NEG_INIT, F32), jnp.zeros((bp, HEADS, QK, HD), F32),
             jnp.zeros((bp, S_GROUPS, S_STATE, S_GROUP_LANES), F32),
             jnp.zeros((bp, CONV_W - 1, S_CONV_DIM), F32), jnp.zeros((bp, HEADS, HD, HD), F32))
    cos_p, sin_p = _rope_tables(np.arange(tp))
    y_prompt, p_st = _run_group(x_prompt, cos_p, sin_p, [empty] * depth, layer_weights,
                                carry=True, tm=tm_p, lc=lc_p)

    bs_, ts, _ = x_sample.shape
    lc_s = CHUNK if ts % CHUNK == 0 else ts
    carried = []
    for l in range(depth):
        mm = jnp.pad(state_mlstm_m[l].astype(F32), ((0, 0), (0, LANES - HEADS)))[:, None, :]
        carried.append((state_mlstm_c[l].astype(F32), state_mlstm_n[l].astype(F32), mm,
                        state_ret[l].astype(F32), _ssd_state_to_lanes(state_ssd[l].astype(F32)),
                        cache_ssd_conv[l].astype(F32), state_hgrn[l].astype(F32)))
    cos_s, sin_s = _rope_tables(np.tile(PAST_LEN + np.arange(ts), bs_))
    y_sample, s_st = _run_group(x_sample.reshape(1, bs_ * ts, D_MODEL), cos_s, sin_s, carried,
                                layer_weights, carry=False, tm=bs_ * ts, lc=lc_s)
    y_sample = y_sample.reshape(bs_, ts, D_MODEL)

    return (y_prompt, y_sample, *p_st, *s_st)
```
